```python
import jax, jax.numpy as jnp
from jax import lax
import numpy as np

D_MODEL = 2048
BATCH = 4
SEQ = 8192
DEPTH = 4
DEC_BATCH = 4
DEC_SEQ = 2048
PAST_LEN = 128

HEAD_DIM = 128
BLOCK = 128
GRID_W = 64
EPS = 1e-6
A_WIDTH = D_MODEL // 2
A_GROUPS = A_WIDTH // HEAD_DIM
A_CHUNK = 128
B_WIDTH = D_MODEL // 2
B_HEADS = B_WIDTH // HEAD_DIM
B_KV_HEADS = 2
B_WINDOW = 128
ROPE_THETA = 500000.0
ROPE_DIMS = HEAD_DIM // 4
C_WIDTH = D_MODEL
C_HEADS = C_WIDTH // HEAD_DIM
C_KV_HEADS = 4
AXIAL_THETA = 10000.0
N_EVEN = (DEPTH + 1) // 2
N_ODD = DEPTH // 2
AB_IN = 3 * A_WIDTH + 2 * B_WIDTH + 2 * B_KV_HEADS * HEAD_DIM
C_IN = 2 * C_WIDTH + 2 * C_KV_HEADS * HEAD_DIM

kernel_name = "hybrid_gmlp_window_axial_encoder"


def _split_points(sizes):
    pts, acc = [], 0
    for s in sizes[:-1]:
        acc += s
        pts.append(acc)
    return pts


def rms_norm(x, g):
    xf = x.astype(jnp.float32)
    y = xf * lax.rsqrt(jnp.mean(xf * xf, axis=-1, keepdims=True) + EPS)
    return (y * g.astype(jnp.float32)).astype(x.dtype)


def rope(x, pos, theta):
    d = x.shape[-1]
    inv = jnp.power(jnp.float32(theta), -jnp.arange(d // 2, dtype=jnp.float32) * (2.0 / d))
    ang = pos[:, None] * inv[None, :]
    cos = jnp.cos(ang)[:, None, :]
    sin = jnp.sin(ang)[:, None, :]
    xf = x.astype(jnp.float32)
    x1, x2 = xf[..., : d // 2], xf[..., d // 2:]
    out = jnp.concatenate([x1 * cos - x2 * sin, x2 * cos + x1 * sin], axis=-1)
    return out.astype(x.dtype)


def partial_rope(x, pos):
    return jnp.concatenate([rope(x[..., :ROPE_DIMS], pos, ROPE_THETA), x[..., ROPE_DIMS:]], axis=-1)


def axial_rope(x, row, col):
    half = x.shape[-1] // 2
    return jnp.concatenate([rope(x[..., :half], row, AXIAL_THETA), rope(x[..., half:], col, AXIAL_THETA)], axis=-1)


def mixer_a(u, v, v_norm, w_s, b_s):
    bn, s, _ = v.shape
    nc = s // A_CHUNK
    vn = rms_norm(v.reshape(bn, s, A_GROUPS, HEAD_DIM), v_norm.reshape(A_GROUPS, HEAD_DIM))
    vc = vn.reshape(bn, nc, A_CHUNK, A_GROUPS, HEAD_DIM)
    mixed = jnp.einsum('gpq,bcqgd->bcpgd', w_s, vc) + b_s.T[None, None, :, :, None]
    return u * mixed.reshape(bn, s, A_WIDTH)


def mixer_b(q, k, v, q_norm, k_norm, sink, pos):
    bn, s, _ = q.shape
    nb = s // BLOCK
    rep = B_HEADS // B_KV_HEADS
    q = partial_rope(rms_norm(q.reshape(bn, s, B_HEADS, HEAD_DIM), q_norm), pos)
    k = partial_rope(rms_norm(k.reshape(bn, s, B_KV_HEADS, HEAD_DIM), k_norm), pos)
    v = v.reshape(bn, s, B_KV_HEADS, HEAD_DIM)
    qb = q.reshape(bn, nb, BLOCK, B_KV_HEADS, rep, HEAD_DIM)
    pad = ((0, 0), (BLOCK, BLOCK), (0, 0), (0, 0))
    kp = jnp.pad(k, pad)
    vp = jnp.pad(v, pad)
    idx = jnp.arange(nb)[:, None] * BLOCK + jnp.arange(3 * BLOCK)[None, :]
    kb = kp[:, idx]
    vb = vp[:, idx]
    scale = HEAD_DIM ** -0.5
    sc = jnp.einsum('bnqhrd,bnkhd->bnhrqk', qb, kb).astype(jnp.float32) * scale
    qpos = jnp.arange(nb)[:, None] * BLOCK + jnp.arange(BLOCK)[None, :]
    kpos = idx - BLOCK
    mask = (jnp.abs(qpos[:, :, None] - kpos[:, None, :]) <= B_WINDOW) & (kpos[:, None, :] >= 0) & (kpos[:, None, :] < s)
    sc = jnp.where(mask[None, :, None, None, :, :], sc, jnp.float32(-1e30))
    sink_l = jnp.broadcast_to(sink.astype(jnp.float32).reshape(B_KV_HEADS, rep)[None, None, :, :, None, None],
                              sc.shape[:-1] + (1,))
    p = jax.nn.softmax(jnp.concatenate([sc, sink_l], axis=-1), axis=-1)[..., :-1]
    o = jnp.einsum('bnhrqk,bnkhd->bnqhrd', p.astype(vb.dtype), vb)
    return o.reshape(bn, s, B_WIDTH)


def mixer_c(q, k, v, q_norm, k_norm, row, col):
    bn, s, _ = q.shape
    nb = s // BLOCK
    rep = C_HEADS // C_KV_HEADS
    q = axial_rope(rms_norm(q.reshape(bn, s, C_HEADS, HEAD_DIM), q_norm), row, col)
    k = axial_rope(rms_norm(k.reshape(bn, s, C_KV_HEADS, HEAD_DIM), k_norm), row, col)
    v = v.reshape(bn, s, C_KV_HEADS, HEAD_DIM)
    qb = q.reshape(bn, nb, BLOCK, C_KV_HEADS, rep, HEAD_DIM).transpose(1, 0, 2, 3, 4, 5)
    scale = HEAD_DIM ** -0.5

    def attend(qblk):
        sc = jnp.einsum('bqhrd,bkhd->bhrqk', qblk, k).astype(jnp.float32) * scale
        p = jax.nn.softmax(sc, axis=-1).astype(v.dtype)
        return jnp.einsum('bhrqk,bkhd->bqhrd', p, v)

    o = lax.map(attend, qb)
    return o.transpose(1, 0, 2, 3, 4, 5).reshape(bn, s, C_WIDTH)


def even_layer(x, norm_g, w_in, w_out, a_vn, a_ws, a_bs, b_qn, b_kn, b_sink, pos):
    h = rms_norm(x, norm_g)
    z = h @ w_in
    sizes = [A_WIDTH, A_WIDTH, A_WIDTH, B_WIDTH, B_KV_HEADS * HEAD_DIM, B_KV_HEADS * HEAD_DIM, B_WIDTH]
    a_u, a_v, a_g, b_q, b_k, b_v, b_g = jnp.split(z, _split_points(sizes), axis=-1)
    ya = mixer_a(jax.nn.gelu(a_u), jax.nn.gelu(a_v), a_vn, a_ws, a_bs) * jax.nn.silu(a_g)
    yb = mixer_b(b_q, b_k, b_v, b_qn, b_kn, b_sink, pos) * jax.nn.silu(b_g)
    return x + jnp.concatenate([ya, yb], axis=-1) @ w_out


def odd_layer(x, norm_g, w_in, w_out, c_qn, c_kn, row, col):
    h = rms_norm(x, norm_g)
    z = h @ w_in
    sizes = [C_WIDTH, C_KV_HEADS * HEAD_DIM, C_KV_HEADS * HEAD_DIM, C_WIDTH]
    c_q, c_k, c_v, c_g = jnp.split(z, _split_points(sizes), axis=-1)
    yc = mixer_c(c_q, c_k, c_v, c_qn, c_kn, row, col) * jax.nn.silu(c_g)
    return x + yc @ w_out


def trunk(x, norm_ab, w_in_ab, w_out_ab, a_v_norm, a_w_s, a_b_s, b_q_norm, b_k_norm, b_sink,
          norm_c, w_in_c, w_out_c, c_q_norm, c_k_norm):
    s = x.shape[1]
    rows = s // GRID_W
    pos = jnp.arange(s, dtype=jnp.float32)
    rr, cc = jnp.meshgrid(jnp.arange(rows, dtype=jnp.float32), jnp.arange(GRID_W, dtype=jnp.float32), indexing='ij')
    row, col = rr.reshape(-1), cc.reshape(-1)
    for layer in range(DEPTH):
        i = layer // 2
        if layer % 2 == 0:
            x = even_layer(x, norm_ab[i], w_in_ab[i], w_out_ab[i], a_v_norm[i], a_w_s[i], a_b_s[i],
                           b_q_norm[i], b_k_norm[i], b_sink[i], pos)
        else:
            x = odd_layer(x, norm_c[i], w_in_c[i], w_out_c[i], c_q_norm[i], c_k_norm[i], row, col)
    return x


def setup_inputs(seed: int = 0) -> dict:
    key = jax.random.key(seed)
    ks = jax.random.split(key, 20)
    f32 = jnp.float32
    nrm = lambda k, shape, sc: jax.random.normal(k, shape, f32) * sc
    gain = lambda k, shape: 1.0 + 0.02 * jax.random.normal(k, shape, f32)
    return {
        "x_prompt": jax.random.normal(ks[0], (BATCH, SEQ, D_MODEL), f32),
        "x_sample": jax.random.normal(ks[1], (DEC_BATCH, DEC_SEQ, D_MODEL), f32),
        "norm_ab": gain(ks[2], (N_EVEN, D_MODEL)),
        "w_in_ab": nrm(ks[3], (N_EVEN, D_MODEL, AB_IN), D_MODEL ** -0.5),
        "w_out_ab": nrm(ks[4], (N_EVEN, A_WIDTH + B_WIDTH, D_MODEL), 0.5 * (A_WIDTH + B_WIDTH) ** -0.5),
        "a_v_norm": gain(ks[5], (N_EVEN, A_WIDTH)),
        "a_w_s": nrm(ks[6], (N_EVEN, A_GROUPS, A_CHUNK, A_CHUNK), 0.5 * A_CHUNK ** -0.5),
        "a_b_s": 1.0 + nrm(ks[7], (N_EVEN, A_GROUPS, A_CHUNK), 0.01),
        "b_q_norm": gain(ks[8], (N_EVEN, HEAD_DIM)),
        "b_k_norm": gain(ks[9], (N_EVEN, HEAD_DIM)),
        "b_sink": nrm(ks[10], (N_EVEN, B_HEADS), 0.5),
        "norm_c": gain(ks[11], (N_ODD, D_MODEL)),
        "w_in_c": nrm(ks[12], (N_ODD, D_MODEL, C_IN), D_MODEL ** -0.5),
        "w_out_c": nrm(ks[13], (N_ODD, C_WIDTH, D_MODEL), 0.5 * C_WIDTH ** -0.5),
        "c_q_norm": gain(ks[14], (N_ODD, HEAD_DIM)),
        "c_k_norm": gain(ks[15], (N_ODD, HEAD_DIM)),
    }


def reference(x_prompt, x_sample, norm_ab, w_in_ab, w_out_ab, a_v_norm, a_w_s, a_b_s, b_q_norm, b_k_norm,
              b_sink, norm_c, w_in_c, w_out_c, c_q_norm, c_k_norm):
    y_prompt = trunk(x_prompt, norm_ab, w_in_ab, w_out_ab, a_v_norm, a_w_s, a_b_s, b_q_norm, b_k_norm, b_sink,
                     norm_c, w_in_c, w_out_c, c_q_norm, c_k_norm)
    y_sample = trunk(x_sample, norm_ab, w_in_ab, w_out_ab, a_v_norm, a_w_s, a_b_s, b_q_norm, b_k_norm, b_sink,
                     norm_c, w_in_c, w_out_c, c_q_norm, c_k_norm)
    return (y_prompt, y_sample)
```

```python
import functools

import jax
import jax.numpy as jnp
from jax import lax
from jax.experimental import pallas as pl
from jax.experimental.pallas import tpu as pltpu

D_MODEL = 2048
HEAD_DIM = 128
BLOCK = 128
GRID_W = 64
EPS = 1e-6
A_WIDTH = D_MODEL // 2
A_GROUPS = A_WIDTH // HEAD_DIM
B_WIDTH = D_MODEL // 2
B_HEADS = B_WIDTH // HEAD_DIM
B_KV_HEADS = 2
B_REP = B_HEADS // B_KV_HEADS
ROPE_THETA = 500000.0
ROPE_DIMS = HEAD_DIM // 4
C_WIDTH = D_MODEL
C_HEADS = C_WIDTH // HEAD_DIM
C_KV_HEADS = 4
C_REP = C_HEADS // C_KV_HEADS
AXIAL_THETA = 10000.0
AB_IN = 3 * A_WIDTH + 2 * B_WIDTH + 2 * B_KV_HEADS * HEAD_DIM
C_IN = 2 * C_WIDTH + 2 * C_KV_HEADS * HEAD_DIM
NEG_BIG = -1e30

VMEM_LIMIT_BYTES = 56 * 1024 * 1024

F32 = jnp.float32
BF16 = jnp.bfloat16

_EVEN_U, _EVEN_V, _EVEN_GA, _EVEN_Q, _EVEN_GB = 0, 1, 2, 3, 4
_EVEN_K = (4 * A_WIDTH + B_WIDTH) // (B_KV_HEADS * HEAD_DIM)
_EVEN_VB = _EVEN_K + 1
_ODD_K_BLK512 = C_WIDTH // 512
_ODD_K_BLK128 = C_WIDTH // HEAD_DIM
_ODD_V_BLK128 = (C_WIDTH + C_KV_HEADS * HEAD_DIM) // HEAD_DIM
_ODD_G_BLK512 = (C_WIDTH + 2 * C_KV_HEADS * HEAD_DIM) // 512


def _params(*sem):
    return pltpu.CompilerParams(dimension_semantics=sem, vmem_limit_bytes=VMEM_LIMIT_BYTES)


def _gelu(x):
    return 0.5 * x * (1.0 + jnp.tanh(0.7978845608028654 * (x + 0.044715 * (x * x * x))))


def _silu(x):
    return x / (1.0 + jnp.exp(-x))


def _head_norm(x, gain):
    ms = jnp.mean(x * x, axis=-1, keepdims=True)
    return x * lax.rsqrt(ms + EPS) * gain


def _rotate(x, cos, sin, half):
    lane = lax.broadcasted_iota(jnp.int32, x.shape, 1)
    up = pltpu.roll(x, HEAD_DIM - half, axis=1)
    down = pltpu.roll(x, half, axis=1)
    rot = jnp.where((lane & (2 * half - 1)) < half, up, down)
    return x * cos + rot * sin


def _norm_matmul_kernel(x_ref, g_ref, w_ref, o_ref, h_ref):
    @pl.when(pl.program_id(1) == 0)
    def _():
        x = x_ref[...]
        ms = jnp.mean(x * x, axis=-1, keepdims=True)
        h_ref[...] = (x * lax.rsqrt(ms + EPS) * g_ref[...]).astype(BF16)

    o_ref[...] = jnp.dot(h_ref[...], w_ref[...], preferred_element_type=F32).astype(o_ref.dtype)


def _norm_matmul(x, gain, w, *, tm=1024, tn=512):
    t, d = x.shape
    n = w.shape[1]
    tm = min(tm, t)
    assert t % tm == 0 and n % tn == 0
    return pl.pallas_call(
        _norm_matmul_kernel,
        out_shape=jax.ShapeDtypeStruct((t, n), BF16),
        grid=(t // tm, n // tn),
        in_specs=[
            pl.BlockSpec((tm, d), lambda i, j: (i, 0)),
            pl.BlockSpec((1, d), lambda i, j: (0, 0)),
            pl.BlockSpec((d, tn), lambda i, j: (0, j)),
        ],
        out_specs=pl.BlockSpec((tm, tn), lambda i, j: (i, j)),
        scratch_shapes=[pltpu.VMEM((tm, d), BF16)],
        compiler_params=_params("parallel", "arbitrary"),
        name="norm_matmul",
    )(x, gain, w)


def _matmul_residual_kernel(y_ref, w_ref, x_ref, o_ref):
    o_ref[...] = x_ref[...] + jnp.dot(y_ref[...], w_ref[...], preferred_element_type=F32)


def _matmul_residual(y, w, x, *, tm=1024, tn=512):
    t, k = y.shape
    n = w.shape[1]
    tm = min(tm, t)
    assert t % tm == 0 and n % tn == 0
    return pl.pallas_call(
        _matmul_residual_kernel,
        out_shape=jax.ShapeDtypeStruct((t, n), F32),
        grid=(t // tm, n // tn),
        in_specs=[
            pl.BlockSpec((tm, k), lambda i, j: (i, 0)),
            pl.BlockSpec((k, tn), lambda i, j: (0, j)),
            pl.BlockSpec((tm, tn), lambda i, j: (i, j)),
        ],
        out_specs=pl.BlockSpec((tm, tn), lambda i, j: (i, j)),
        compiler_params=_params("parallel", "arbitrary"),
        name="matmul_residual",
    )(y, w, x)


def _even_mixer_kernel(u_ref, v_ref, ga_ref, q_ref, gb_ref,
                       kp_ref, kc_ref, kn_ref, vp_ref, vc_ref, vn_ref,
                       cosp_ref, cosc_ref, cosn_ref, sinp_ref, sinc_ref, sinn_ref,
                       avn_ref, ws_ref, bs_ref, qg_ref, kg_ref, sink_ref,
                       o_ref, *, nb):
    t = pl.program_id(1)

    u = _gelu(u_ref[...].astype(F32))
    v = _gelu(v_ref[...].astype(F32))
    ga = _silu(ga_ref[...].astype(F32))
    for g in range(A_GROUPS):
        sl = slice(g * HEAD_DIM, (g + 1) * HEAD_DIM)
        vn = _head_norm(v[:, sl], avn_ref[:, sl]).astype(BF16)
        mixed = jnp.dot(ws_ref[g], vn, preferred_element_type=F32) + bs_ref[g]
        o_ref[:, sl] = (u[:, sl] * mixed * ga[:, sl]).astype(o_ref.dtype)

    scale = HEAD_DIM ** -0.5
    cos_q, sin_q = cosc_ref[...], sinc_ref[...]
    cos_k = jnp.concatenate([cosp_ref[...], cos_q, cosn_ref[...]], axis=0)
    sin_k = jnp.concatenate([sinp_ref[...], sin_q, sinn_ref[...]], axis=0)
    k_win = jnp.concatenate([kp_ref[...], kc_ref[...], kn_ref[...]], axis=0).astype(F32)
    v_win = jnp.concatenate([vp_ref[...], vc_ref[...], vn_ref[...]], axis=0)
    q_all = q_ref[...].astype(F32)
    gb = _silu(gb_ref[...].astype(F32))

    qi = lax.broadcasted_iota(jnp.int32, (BLOCK, 3 * BLOCK), 0)
    kj = lax.broadcasted_iota(jnp.int32, (BLOCK, 3 * BLOCK), 1)
    band = (kj - qi >= 0) & (kj - qi <= 2 * BLOCK)
    first_key = jnp.where(t > 0, 0, BLOCK)
    end_key = jnp.where(t < nb - 1, 3 * BLOCK, 2 * BLOCK)
    mask = band & (kj >= first_key) & (kj < end_key)

    for h in range(B_KV_HEADS):
        ksl = slice(h * HEAD_DIM, (h + 1) * HEAD_DIM)
        k_h = _rotate(_head_norm(k_win[:, ksl], kg_ref[...]), cos_k, sin_k, ROPE_DIMS // 2).astype(BF16)
        q_rows = []
        for r in range(B_REP):
            sl = slice((h * B_REP + r) * HEAD_DIM, (h * B_REP + r + 1) * HEAD_DIM)
            q_h = _rotate(_head_norm(q_all[:, sl], qg_ref[...]), cos_q, sin_q, ROPE_DIMS // 2)
            q_rows.append((q_h * scale).astype(BF16))
        q_stack = jnp.concatenate(q_rows, axis=0)
        s = lax.dot_general(q_stack, k_h, (((1,), (1,)), ((), ())),
                            preferred_element_type=F32)
        p_rows, inv_rows = [], []
        for r in range(B_REP):
            sink = sink_ref[h * B_REP + r]
            s_r = jnp.where(mask, s[r * BLOCK:(r + 1) * BLOCK, :], NEG_BIG)
            m = jnp.maximum(jnp.max(s_r, axis=1, keepdims=True), sink)
            p = jnp.exp(s_r - m)
            denom = jnp.sum(p, axis=1, keepdims=True) + jnp.exp(sink - m)
            p_rows.append(p.astype(BF16))
            inv_rows.append(1.0 / denom)
        p_stack = jnp.concatenate(p_rows, axis=0)
        o = jnp.dot(p_stack, v_win[:, ksl], preferred_element_type=F32)
        for r in range(B_REP):
            col = (h * B_REP + r) * HEAD_DIM
            o_r = o[r * BLOCK:(r + 1) * BLOCK, :] * inv_rows[r]
            o_ref[:, A_WIDTH + col:A_WIDTH + col + HEAD_DIM] = (
                o_r * gb[:, col:col + HEAD_DIM]).astype(o_ref.dtype)


def _even_mixer(z, cos, sin, a_vn, a_ws, a_bs, q_gain, k_gain, sink):
    b, s, _ = z.shape
    nb = s // BLOCK
    wide = lambda c: pl.BlockSpec((None, BLOCK, A_WIDTH), lambda bi, ti: (bi, ti, c))
    kv_w = B_KV_HEADS * HEAD_DIM
    prev = lambda ti: jnp.maximum(ti - 1, 0)
    nxt = lambda ti: jnp.minimum(ti + 1, nb - 1)
    kv = lambda c, f: pl.BlockSpec((None, BLOCK, kv_w), lambda bi, ti: (bi, f(ti), c))
    tab = lambda f: pl.BlockSpec((BLOCK, HEAD_DIM), lambda bi, ti: (f(ti), 0))
    same = lambda ti: ti
    full = lambda shape: pl.BlockSpec(shape, lambda bi, ti: (0,) * len(shape))
    in_specs = [
        wide(_EVEN_U), wide(_EVEN_V), wide(_EVEN_GA), wide(_EVEN_Q), wide(_EVEN_GB),
        kv(_EVEN_K, prev), kv(_EVEN_K, same), kv(_EVEN_K, nxt),
        kv(_EVEN_VB, prev), kv(_EVEN_VB, same), kv(_EVEN_VB, nxt),
        tab(prev), tab(same), tab(nxt), tab(prev), tab(same), tab(nxt),
        full((1, A_WIDTH)), full((A_GROUPS, BLOCK, BLOCK)), full((A_GROUPS, BLOCK, HEAD_DIM)),
        full((1, HEAD_DIM)), full((1, HEAD_DIM)),
        pl.BlockSpec(memory_space=pltpu.SMEM),
    ]
    return pl.pallas_call(
        functools.partial(_even_mixer_kernel, nb=nb),
        out_shape=jax.ShapeDtypeStruct((b, s, A_WIDTH + B_WIDTH), BF16),
        grid=(b, nb),
        in_specs=in_specs,
        out_specs=pl.BlockSpec((None, BLOCK, A_WIDTH + B_WIDTH), lambda bi, ti: (bi, ti, 0)),
        compiler_params=_params("parallel", "parallel"),
        name="even_mixer",
    )(z, z, z, z, z, z, z, z, z, z, z, cos, cos, cos, sin, sin, sin,
      a_vn, a_ws, a_bs, q_gain, k_gain, sink)


def _k_prep_kernel(k_ref, cos_ref, sin_ref, g_ref, o_ref):
    k = k_ref[...].astype(F32)
    cos, sin = cos_ref[...], sin_ref[...]
    for h in range(C_KV_HEADS):
        sl = slice(h * HEAD_DIM, (h + 1) * HEAD_DIM)
        o_ref[:, sl] = _rotate(_head_norm(k[:, sl], g_ref[...]), cos, sin, HEAD_DIM // 4).astype(o_ref.dtype)


def _k_prep(z, cos, sin, k_gain, *, ts=512):
    b, s, _ = z.shape
    ts = min(ts, s)
    kw = C_KV_HEADS * HEAD_DIM
    return pl.pallas_call(
        _k_prep_kernel,
        out_shape=jax.ShapeDtypeStruct((b, s, kw), BF16),
        grid=(b, s // ts),
        in_specs=[
            pl.BlockSpec((None, ts, kw), lambda bi, ti: (bi, ti, _ODD_K_BLK512)),
            pl.BlockSpec((ts, HEAD_DIM), lambda bi, ti: (ti, 0)),
            pl.BlockSpec((ts, HEAD_DIM), lambda bi, ti: (ti, 0)),
            pl.BlockSpec((1, HEAD_DIM), lambda bi, ti: (0, 0)),
        ],
        out_specs=pl.BlockSpec((None, ts, kw), lambda bi, ti: (bi, ti, 0)),
        compiler_params=_params("parallel", "parallel"),
        name="k_prep",
    )(z, cos, sin, k_gain)


def _flash_kernel(q_ref, k_ref, v_ref, g_ref, cos_ref, sin_ref, qg_ref, o_ref,
                  qs_ref, m_ref, l_ref, acc_ref, *, tq, tk, n_kv):
    scale = HEAD_DIM ** -0.5
    q = q_ref[...].astype(F32)
    cos, sin = cos_ref[...], sin_ref[...]
    for r in range(C_REP):
        sl = slice(r * HEAD_DIM, (r + 1) * HEAD_DIM)
        q_h = _rotate(_head_norm(q[:, sl], qg_ref[...]), cos, sin, HEAD_DIM // 4)
        qs_ref[r * tq:(r + 1) * tq, :] = (q_h * scale).astype(BF16)
    m_ref[...] = jnp.full(m_ref.shape, NEG_BIG, F32)
    l_ref[...] = jnp.zeros(l_ref.shape, F32)
    acc_ref[...] = jnp.zeros(acc_ref.shape, F32)
    n_lane_tiles = tk // HEAD_DIM

    def body(j, carry):
        start = pl.multiple_of(j * tk, tk)
        k_blk = k_ref[pl.ds(start, tk), :]
        v_blk = v_ref[pl.ds(start, tk), :]
        s = lax.dot_general(qs_ref[...], k_blk, (((1,), (1,)), ((), ())),
                            preferred_element_type=F32)
        tiles = [s[:, c * HEAD_DIM:(c + 1) * HEAD_DIM] for c in range(n_lane_tiles)]
        m_part = functools.reduce(jnp.maximum, tiles)
        m_prev = m_ref[...]
        m_new = jnp.maximum(m_prev, jnp.max(m_part, axis=1, keepdims=True))
        alpha = jnp.exp(m_prev - m_new)
        p_tiles = [jnp.exp(tile - m_new) for tile in tiles]
        l_ref[...] = alpha * l_ref[...] + functools.reduce(jnp.add, p_tiles)
        p = jnp.concatenate([pt.astype(BF16) for pt in p_tiles], axis=1)
        acc_ref[...] = alpha * acc_ref[...] + jnp.dot(p, v_blk, preferred_element_type=F32)
        m_ref[...] = m_new
        return carry

    lax.fori_loop(0, n_kv, body, 0)

    inv_l = 1.0 / jnp.sum(l_ref[...], axis=1, keepdims=True)
    gate = _silu(g_ref[...].astype(F32))
    for r in range(C_REP):
        sl = slice(r * HEAD_DIM, (r + 1) * HEAD_DIM)
        o_r = acc_ref[r * tq:(r + 1) * tq, :] * inv_l[r * tq:(r + 1) * tq, :]
        o_ref[:, sl] = (o_r * gate[:, sl]).astype(o_ref.dtype)


def _flash(z, k_prepped, cos, sin, q_gain, *, tq=256, tk=512):
    b, s, _ = z.shape
    tq, tk = min(tq, s), min(tk, s)
    assert s % tq == 0 and s % tk == 0
    qw = C_REP * HEAD_DIM
    rows = C_REP * tq
    return pl.pallas_call(
        functools.partial(_flash_kernel, tq=tq, tk=tk, n_kv=s // tk),
        out_shape=jax.ShapeDtypeStruct((b, s, C_WIDTH), BF16),
        grid=(b, C_KV_HEADS, s // tq),
        in_specs=[
            pl.BlockSpec((None, tq, qw), lambda bi, hi, qi: (bi, qi, hi)),
            pl.BlockSpec((None, s, HEAD_DIM), lambda bi, hi, qi: (bi, 0, hi)),
            pl.BlockSpec((None, s, HEAD_DIM), lambda bi, hi, qi: (bi, 0, _ODD_V_BLK128 + hi)),
            pl.BlockSpec((None, tq, qw), lambda bi, hi, qi: (bi, qi, _ODD_G_BLK512 + hi)),
            pl.BlockSpec((tq, HEAD_DIM), lambda bi, hi, qi: (qi, 0)),
            pl.BlockSpec((tq, HEAD_DIM), lambda bi, hi, qi: (qi, 0)),
            pl.BlockSpec((1, HEAD_DIM), lambda bi, hi, qi: (0, 0)),
        ],
        out_specs=pl.BlockSpec((None, tq, qw), lambda bi, hi, qi: (bi, qi, hi)),
        scratch_shapes=[
            pltpu.VMEM((rows, HEAD_DIM), BF16),
            pltpu.VMEM((rows, HEAD_DIM), F32),
            pltpu.VMEM((rows, HEAD_DIM), F32),
            pltpu.VMEM((rows, HEAD_DIM), F32),
        ],
        compiler_params=_params("parallel", "parallel", "arbitrary"),
        name="flash_gqa",
    )(z, k_prepped, z, z, cos, sin, q_gain)


def _rope_tables(pos_groups, theta, half, s):
    inv = jnp.power(F32(theta), -jnp.arange(half, dtype=F32) * (1.0 / half))
    cos = jnp.ones((s, HEAD_DIM), F32)
    sin = jnp.zeros((s, HEAD_DIM), F32)
    for pos, off in pos_groups:
        ang = pos[:, None] * inv[None, :]
        c, sn = jnp.cos(ang), jnp.sin(ang)
        cos = cos.at[:, off:off + 2 * half].set(jnp.concatenate([c, c], axis=1))
        sin = sin.at[:, off:off + 2 * half].set(jnp.concatenate([-sn, sn], axis=1))
    return cos, sin


def _trunk(x, tables, even_w, odd_w):
    b, s, d = x.shape
    (cos_b, sin_b), (cos_c, sin_c) = tables
    cos_b, sin_b, cos_c, sin_c = cos_b[:s], sin_b[:s], cos_c[:s], sin_c[:s]
    xf = x.reshape(b * s, d)
    depth = len(even_w) + len(odd_w)
    for layer in range(depth):
        i = layer // 2
        if layer % 2 == 0:
            norm_g, w_in, w_out, a_vn, a_ws, a_bs, qg, kg, sink = even_w[i]
            z = _norm_matmul(xf, norm_g, w_in).reshape(b, s, AB_IN)
            y = _even_mixer(z, cos_b, sin_b, a_vn, a_ws, a_bs, qg, kg, sink)
        else:
            norm_g, w_in, w_out, qg, kg = odd_w[i]
            z = _norm_matmul(xf, norm_g, w_in).reshape(b, s, C_IN)
            kp = _k_prep(z, cos_c, sin_c, kg)
            y = _flash(z, kp, cos_c, sin_c, qg)
        xf = _matmul_residual(y.reshape(b * s, d), w_out, xf)
    return xf.reshape(b, s, d)


def kernel(x_prompt, x_sample, norm_ab, w_in_ab, w_out_ab, a_v_norm, a_w_s, a_b_s, b_q_norm, b_k_norm,
           b_sink, norm_c, w_in_c, w_out_c, c_q_norm, c_k_norm):
    s_max = max(x_prompt.shape[1], x_sample.shape[1])
    pos = jnp.arange(s_max, dtype=F32)
    row = jnp.floor(pos / GRID_W)
    col = pos - row * GRID_W
    tables = (
        _rope_tables([(pos, 0)], ROPE_THETA, ROPE_DIMS // 2, s_max),
        _rope_tables([(row, 0), (col, HEAD_DIM // 2)], AXIAL_THETA, HEAD_DIM // 4, s_max),
    )

    c0 = 3 * A_WIDTH + B_WIDTH
    c1 = c0 + 2 * B_KV_HEADS * HEAD_DIM
    even_w = []
    for i in range(norm_ab.shape[0]):
        w = w_in_ab[i]
        w_perm = jnp.concatenate([w[:, :c0], w[:, c1:], w[:, c0:c1]], axis=1).astype(BF16)
        even_w.append((
            norm_ab[i][None, :], w_perm, w_out_ab[i].astype(BF16),
            a_v_norm[i][None, :], a_w_s[i].astype(BF16),
            jnp.broadcast_to(a_b_s[i][:, :, None], (A_GROUPS, BLOCK, HEAD_DIM)),
            b_q_norm[i][None, :], b_k_norm[i][None, :], b_sink[i],
        ))
    odd_w = []
    for i in range(norm_c.shape[0]):
        odd_w.append((norm_c[i][None, :], w_in_c[i].astype(BF16), w_out_c[i].astype(BF16),
                      c_q_norm[i][None, :], c_k_norm[i][None, :]))

    y_prompt = _trunk(x_prompt, tables, even_w, odd_w)
    y_sample = _trunk(x_sample, tables, even_w, odd_w)
    return (y_prompt, y_sample)
```

```python
import functools

import jax
import jax.numpy as jnp
from jax import lax
from jax.experimental import pallas as pl
from jax.experimental.pallas import tpu as pltpu

D_MODEL = 2048
HEAD_DIM = 128
BLOCK = 128
GRID_W = 64
EPS = 1e-6
A_WIDTH = D_MODEL // 2
A_GROUPS = A_WIDTH // HEAD_DIM
B_WIDTH = D_MODEL // 2
B_HEADS = B_WIDTH // HEAD_DIM
B_KV_HEADS = 2
B_REP = B_HEADS // B_KV_HEADS
ROPE_THETA = 500000.0
ROPE_DIMS = HEAD_DIM // 4
C_WIDTH = D_MODEL
C_HEADS = C_WIDTH // HEAD_DIM
C_KV_HEADS = 4
C_REP = C_HEADS // C_KV_HEADS
AXIAL_THETA = 10000.0
AB_IN = 3 * A_WIDTH + 2 * B_WIDTH + 2 * B_KV_HEADS * HEAD_DIM
C_IN = 2 * C_WIDTH + 2 * C_KV_HEADS * HEAD_DIM
NEG_BIG = -1e30
LOG2_E = 1.4426950408889634

VMEM_LIMIT_BYTES = 56 * 1024 * 1024

F32 = jnp.float32
BF16 = jnp.bfloat16

_EVEN_U, _EVEN_V, _EVEN_GA, _EVEN_Q, _EVEN_GB = 0, 1, 2, 3, 4
_EVEN_K = (4 * A_WIDTH + B_WIDTH) // (B_KV_HEADS * HEAD_DIM)
_EVEN_VB = _EVEN_K + 1
_ODD_K_BLK512 = C_WIDTH // 512
_ODD_K_BLK128 = C_WIDTH // HEAD_DIM
_ODD_V_BLK128 = (C_WIDTH + C_KV_HEADS * HEAD_DIM) // HEAD_DIM
_ODD_G_BLK512 = (C_WIDTH + 2 * C_KV_HEADS * HEAD_DIM) // 512


def _params(*sem):
    return pltpu.CompilerParams(dimension_semantics=sem, vmem_limit_bytes=VMEM_LIMIT_BYTES)


def _gelu(x):
    return 0.5 * x * (1.0 + jnp.tanh(0.7978845608028654 * (x + 0.044715 * (x * x * x))))


def _silu(x):
    return x / (1.0 + jnp.exp(-x))


def _head_norm(x, gain):
    ms = jnp.mean(x * x, axis=-1, keepdims=True)
    return x * lax.rsqrt(ms + EPS) * gain


def _rotate(x, cos, sin, half):
    lane = lax.broadcasted_iota(jnp.int32, x.shape, 1)
    up = pltpu.roll(x, HEAD_DIM - half, axis=1)
    down = pltpu.roll(x, half, axis=1)
    rot = jnp.where((lane & (2 * half - 1)) < half, up, down)
    return x * cos + rot * sin


def _norm_matmul_kernel(x_ref, g_ref, w_ref, o_ref, h_ref):
    @pl.when(pl.program_id(1) == 0)
    def _():
        x = x_ref[...]
        ms = jnp.mean(x * x, axis=-1, keepdims=True)
        h_ref[...] = (x * lax.rsqrt(ms + EPS) * g_ref[...]).astype(BF16)

    o_ref[...] = jnp.dot(h_ref[...], w_ref[...], preferred_element_type=F32).astype(o_ref.dtype)


def _norm_matmul(x, gain, w, *, tm=1024, tn=512):
    t, d = x.shape
    n = w.shape[1]
    tm = min(tm, t)
    assert t % tm == 0 and n % tn == 0
    return pl.pallas_call(
        _norm_matmul_kernel,
        out_shape=jax.ShapeDtypeStruct((t, n), BF16),
        grid=(t // tm, n // tn),
        in_specs=[
            pl.BlockSpec((tm, d), lambda i, j: (i, 0)),
            pl.BlockSpec((1, d), lambda i, j: (0, 0)),
            pl.BlockSpec((d, tn), lambda i, j: (0, j)),
        ],
        out_specs=pl.BlockSpec((tm, tn), lambda i, j: (i, j)),
        scratch_shapes=[pltpu.VMEM((tm, d), BF16)],
        compiler_params=_params("parallel", "arbitrary"),
        name="norm_matmul",
    )(x, gain, w)


def _matmul_residual_kernel(y_ref, w_ref, x_ref, o_ref):
    o_ref[...] = x_ref[...] + jnp.dot(y_ref[...], w_ref[...], preferred_element_type=F32)


def _matmul_residual(y, w, x, *, tm=512):
    t, k = y.shape
    n = w.shape[1]
    tm = min(tm, t)
    assert t % tm == 0
    return pl.pallas_call(
        _matmul_residual_kernel,
        out_shape=jax.ShapeDtypeStruct((t, n), F32),
        grid=(t // tm,),
        in_specs=[
            pl.BlockSpec((tm, k), lambda i: (i, 0)),
            pl.BlockSpec((k, n), lambda i: (0, 0)),
            pl.BlockSpec((tm, n), lambda i: (i, 0)),
        ],
        out_specs=pl.BlockSpec((tm, n), lambda i: (i, 0)),
        compiler_params=_params("parallel"),
        name="matmul_residual",
    )(y, w, x)


def _even_mixer_kernel(u_ref, v_ref, ga_ref, q_ref, gb_ref,
                       kp_ref, kc_ref, kn_ref, vp_ref, vc_ref, vn_ref,
                       cosp_ref, cosc_ref, cosn_ref, sinp_ref, sinc_ref, sinn_ref,
                       avn_ref, ws_ref, bs_ref, qg_ref, kg_ref, sink_ref,
                       o_ref, *, nb):
    t = pl.program_id(1)

    u = _gelu(u_ref[...].astype(F32))
    v = _gelu(v_ref[...].astype(F32))
    ga = _silu(ga_ref[...].astype(F32))
    for g in range(A_GROUPS):
        sl = slice(g * HEAD_DIM, (g + 1) * HEAD_DIM)
        vn = _head_norm(v[:, sl], avn_ref[:, sl]).astype(BF16)
        mixed = jnp.dot(ws_ref[g], vn, preferred_element_type=F32) + bs_ref[g]
        o_ref[:, sl] = (u[:, sl] * mixed * ga[:, sl]).astype(o_ref.dtype)

    scale = HEAD_DIM ** -0.5
    cos_q, sin_q = cosc_ref[...], sinc_ref[...]
    cos_k = jnp.concatenate([cosp_ref[...], cos_q, cosn_ref[...]], axis=0)
    sin_k = jnp.concatenate([sinp_ref[...], sin_q, sinn_ref[...]], axis=0)
    k_win = jnp.concatenate([kp_ref[...], kc_ref[...], kn_ref[...]], axis=0).astype(F32)
    v_win = jnp.concatenate([vp_ref[...], vc_ref[...], vn_ref[...]], axis=0)
    q_all = q_ref[...].astype(F32)
    gb = _silu(gb_ref[...].astype(F32))

    qi = lax.broadcasted_iota(jnp.int32, (BLOCK, 3 * BLOCK), 0)
    kj = lax.broadcasted_iota(jnp.int32, (BLOCK, 3 * BLOCK), 1)
    band = (kj - qi >= 0) & (kj - qi <= 2 * BLOCK)
    first_key = jnp.where(t > 0, 0, BLOCK)
    end_key = jnp.where(t < nb - 1, 3 * BLOCK, 2 * BLOCK)
    mask = band & (kj >= first_key) & (kj < end_key)

    for h in range(B_KV_HEADS):
        ksl = slice(h * HEAD_DIM, (h + 1) * HEAD_DIM)
        k_h = _rotate(_head_norm(k_win[:, ksl], kg_ref[...]), cos_k, sin_k, ROPE_DIMS // 2).astype(BF16)
        q_rows = []
        for r in range(B_REP):
            sl = slice((h * B_REP + r) * HEAD_DIM, (h * B_REP + r + 1) * HEAD_DIM)
            q_h = _rotate(_head_norm(q_all[:, sl], qg_ref[...]), cos_q, sin_q, ROPE_DIMS // 2)
            q_rows.append((q_h * scale).astype(BF16))
        q_stack = jnp.concatenate(q_rows, axis=0)
        s = lax.dot_general(q_stack, k_h, (((1,), (1,)), ((), ())),
                            preferred_element_type=F32)
        p_rows, inv_rows = [], []
        for r in range(B_REP):
            sink = sink_ref[h * B_REP + r]
            s_r = jnp.where(mask, s[r * BLOCK:(r + 1) * BLOCK, :], NEG_BIG)
            m = jnp.maximum(jnp.max(s_r, axis=1, keepdims=True), sink)
            p = jnp.exp(s_r - m)
            denom = jnp.sum(p, axis=1, keepdims=True) + jnp.exp(sink - m)
            p_rows.append(p.astype(BF16))
            inv_rows.append(1.0 / denom)
        p_stack = jnp.concatenate(p_rows, axis=0)
        o = jnp.dot(p_stack, v_win[:, ksl], preferred_element_type=F32)
        for r in range(B_REP):
            col = (h * B_REP + r) * HEAD_DIM
            o_r = o[r * BLOCK:(r + 1) * BLOCK, :] * inv_rows[r]
            o_ref[:, A_WIDTH + col:A_WIDTH + col + HEAD_DIM] = (
                o_r * gb[:, col:col + HEAD_DIM]).astype(o_ref.dtype)


def _even_mixer(z, cos, sin, a_vn, a_ws, a_bs, q_gain, k_gain, sink):
    b, s, _ = z.shape
    nb = s // BLOCK
    wide = lambda c: pl.BlockSpec((None, BLOCK, A_WIDTH), lambda bi, ti: (bi, ti, c))
    kv_w = B_KV_HEADS * HEAD_DIM
    prev = lambda ti: jnp.maximum(ti - 1, 0)
    nxt = lambda ti: jnp.minimum(ti + 1, nb - 1)
    kv = lambda c, f: pl.BlockSpec((None, BLOCK, kv_w), lambda bi, ti: (bi, f(ti), c))
    tab = lambda f: pl.BlockSpec((BLOCK, HEAD_DIM), lambda bi, ti: (f(ti), 0))
    same = lambda ti: ti
    full = lambda shape: pl.BlockSpec(shape, lambda bi, ti: (0,) * len(shape))
    in_specs = [
        wide(_EVEN_U), wide(_EVEN_V), wide(_EVEN_GA), wide(_EVEN_Q), wide(_EVEN_GB),
        kv(_EVEN_K, prev), kv(_EVEN_K, same), kv(_EVEN_K, nxt),
        kv(_EVEN_VB, prev), kv(_EVEN_VB, same), kv(_EVEN_VB, nxt),
        tab(prev), tab(same), tab(nxt), tab(prev), tab(same), tab(nxt),
        full((1, A_WIDTH)), full((A_GROUPS, BLOCK, BLOCK)), full((A_GROUPS, BLOCK, HEAD_DIM)),
        full((1, HEAD_DIM)), full((1, HEAD_DIM)),
        pl.BlockSpec(memory_space=pltpu.SMEM),
    ]
    return pl.pallas_call(
        functools.partial(_even_mixer_kernel, nb=nb),
        out_shape=jax.ShapeDtypeStruct((b, s, A_WIDTH + B_WIDTH), BF16),
        grid=(b, nb),
        in_specs=in_specs,
        out_specs=pl.BlockSpec((None, BLOCK, A_WIDTH + B_WIDTH), lambda bi, ti: (bi, ti, 0)),
        compiler_params=_params("parallel", "parallel"),
        name="even_mixer",
    )(z, z, z, z, z, z, z, z, z, z, z, cos, cos, cos, sin, sin, sin,
      a_vn, a_ws, a_bs, q_gain, k_gain, sink)


def _qk_prep_kernel(qk_ref, cos_ref, sin_ref, qg_ref, kg_ref, q_out, k_out):
    cos, sin = cos_ref[...], sin_ref[...]
    q_scale = HEAD_DIM ** -0.5 * LOG2_E
    for h in range(C_HEADS):
        x = qk_ref[:, h * HEAD_DIM:(h + 1) * HEAD_DIM].astype(F32)
        q_h = _rotate(_head_norm(x, qg_ref[...]), cos, sin, HEAD_DIM // 4)
        q_out[h] = (q_h * q_scale).astype(q_out.dtype)
    for h in range(C_KV_HEADS):
        x = qk_ref[:, C_WIDTH + h * HEAD_DIM:C_WIDTH + (h + 1) * HEAD_DIM].astype(F32)
        k_out[:, h * HEAD_DIM:(h + 1) * HEAD_DIM] = _rotate(
            _head_norm(x, kg_ref[...]), cos, sin, HEAD_DIM // 4).astype(k_out.dtype)


def _qk_prep(z, cos, sin, q_gain, k_gain, *, ts=256):
    b, s, _ = z.shape
    ts = min(ts, s)
    kw = C_KV_HEADS * HEAD_DIM
    return pl.pallas_call(
        _qk_prep_kernel,
        out_shape=(jax.ShapeDtypeStruct((b, C_HEADS, s, HEAD_DIM), BF16),
                   jax.ShapeDtypeStruct((b, s, kw), BF16)),
        grid=(b, s // ts),
        in_specs=[
            pl.BlockSpec((None, ts, C_WIDTH + kw), lambda bi, ti: (bi, ti, 0)),
            pl.BlockSpec((ts, HEAD_DIM), lambda bi, ti: (ti, 0)),
            pl.BlockSpec((ts, HEAD_DIM), lambda bi, ti: (ti, 0)),
            pl.BlockSpec((1, HEAD_DIM), lambda bi, ti: (0, 0)),
            pl.BlockSpec((1, HEAD_DIM), lambda bi, ti: (0, 0)),
        ],
        out_specs=(pl.BlockSpec((None, C_HEADS, ts, HEAD_DIM), lambda bi, ti: (bi, 0, ti, 0)),
                   pl.BlockSpec((None, ts, kw), lambda bi, ti: (bi, ti, 0))),
        compiler_params=_params("parallel", "parallel"),
        name="qk_prep",
    )(z, cos, sin, q_gain, k_gain)


def _flash_kernel(q_ref, k_ref, v_ref, g_ref, o_ref, s_ref, m_ref, acc_ref, *, tq, tk, n_kv):
    m_ref[...] = jnp.full(m_ref.shape, NEG_BIG, F32)
    acc_ref[...] = jnp.zeros(acc_ref.shape, F32)
    n_lane_tiles = tk // HEAD_DIM
    ones = jnp.ones((tk, HEAD_DIM), BF16)

    def scores(j):
        start = pl.multiple_of(j * tk, tk)
        q_stack = q_ref[...].reshape(C_REP * tq, HEAD_DIM)
        return lax.dot_general(q_stack, k_ref[pl.ds(start, tk), :], (((1,), (1,)), ((), ())),
                               preferred_element_type=F32)

    def softmax_pv(slot, j):
        start = pl.multiple_of(j * tk, tk)
        v_ext = jnp.concatenate([v_ref[pl.ds(start, tk), :], ones], axis=1)
        tiles = [s_ref[slot, :, c * HEAD_DIM:(c + 1) * HEAD_DIM] for c in range(n_lane_tiles)]
        m_prev = m_ref[...]
        m_new = jnp.maximum(m_prev, jnp.max(functools.reduce(jnp.maximum, tiles), axis=1, keepdims=True))
        alpha = jnp.exp2(m_prev - m_new)
        p = jnp.concatenate([jnp.exp2(tile - m_new).astype(BF16) for tile in tiles], axis=1)
        pv = jnp.dot(p, v_ext, preferred_element_type=F32)
        acc_ref[...] = jnp.concatenate([alpha, alpha], axis=1) * acc_ref[...] + pv
        m_ref[...] = m_new

    s_ref[0] = scores(0)

    def body(jj, carry):
        j0 = 2 * jj
        s_ref[1] = scores(j0 + 1)
        softmax_pv(0, j0)
        s_ref[0] = scores(j0 + 2)
        softmax_pv(1, j0 + 1)
        return carry

    lax.fori_loop(0, n_kv // 2 - 1, body, 0)
    s_ref[1] = scores(n_kv - 1)
    softmax_pv(0, n_kv - 2)
    softmax_pv(1, n_kv - 1)

    gate = _silu(g_ref[...].astype(F32))
    for r in range(C_REP):
        sl = slice(r * HEAD_DIM, (r + 1) * HEAD_DIM)
        rows = slice(r * tq, (r + 1) * tq)
        o_r = acc_ref[rows, :HEAD_DIM] / acc_ref[rows, HEAD_DIM:]
        o_ref[:, sl] = (o_r * gate[:, sl]).astype(o_ref.dtype)


def _flash(z, q_prepped, k_prepped, *, tq=256, tk=512):
    b, s, _ = z.shape
    tq, tk = min(tq, s), min(tk, s)
    n_kv = s // tk
    assert s % tq == 0 and s % tk == 0 and n_kv % 2 == 0
    qw = C_REP * HEAD_DIM
    rows = C_REP * tq
    return pl.pallas_call(
        functools.partial(_flash_kernel, tq=tq, tk=tk, n_kv=n_kv),
        out_shape=jax.ShapeDtypeStruct((b, s, C_WIDTH), BF16),
        grid=(b, C_KV_HEADS, s // tq),
        in_specs=[
            pl.BlockSpec((None, C_REP, tq, HEAD_DIM), lambda bi, hi, qi: (bi, hi, qi, 0)),
            pl.BlockSpec((None, s, HEAD_DIM), lambda bi, hi, qi: (bi, 0, hi)),
            pl.BlockSpec((None, s, HEAD_DIM), lambda bi, hi, qi: (bi, 0, _ODD_V_BLK128 + hi)),
            pl.BlockSpec((None, tq, qw), lambda bi, hi, qi: (bi, qi, _ODD_G_BLK512 + hi)),
        ],
        out_specs=pl.BlockSpec((None, tq, qw), lambda bi, hi, qi: (bi, qi, hi)),
        scratch_shapes=[
            pltpu.VMEM((2, rows, tk), F32),
            pltpu.VMEM((rows, HEAD_DIM), F32),
            pltpu.VMEM((rows, 2 * HEAD_DIM), F32),
        ],
        compiler_params=_params("parallel", "parallel", "arbitrary"),
        name="flash_gqa",
    )(q_prepped, k_prepped, z, z)


def _rope_tables(pos_groups, theta, half, s):
    inv = jnp.power(F32(theta), -jnp.arange(half, dtype=F32) * (1.0 / half))
    cos = jnp.ones((s, HEAD_DIM), F32)
    sin = jnp.zeros((s, HEAD_DIM), F32)
    for pos, off in pos_groups:
        ang = pos[:, None] * inv[None, :]
        c, sn = jnp.cos(ang), jnp.sin(ang)
        cos = cos.at[:, off:off + 2 * half].set(jnp.concatenate([c, c], axis=1))
        sin = sin.at[:, off:off + 2 * half].set(jnp.concatenate([-sn, sn], axis=1))
    return cos, sin


def _trunk(x, tables, even_w, odd_w):
    b, s, d = x.shape
    (cos_b, sin_b), (cos_c, sin_c) = tables
    cos_b, sin_b, cos_c, sin_c = cos_b[:s], sin_b[:s], cos_c[:s], sin_c[:s]
    xf = x.reshape(b * s, d)
    depth = len(even_w) + len(odd_w)
    for layer in range(depth):
        i = layer // 2
        if layer % 2 == 0:
            norm_g, w_in, w_out, a_vn, a_ws, a_bs, qg, kg, sink = even_w[i]
            z = _norm_matmul(xf, norm_g, w_in).reshape(b, s, AB_IN)
            y = _even_mixer(z, cos_b, sin_b, a_vn, a_ws, a_bs, qg, kg, sink)
        else:
            norm_g, w_in, w_out, qg, kg = odd_w[i]
            z = _norm_matmul(xf, norm_g, w_in).reshape(b, s, C_IN)
            qp, kp = _qk_prep(z, cos_c, sin_c, qg, kg)
            y = _flash(z, qp, kp)
        xf = _matmul_residual(y.reshape(b * s, d), w_out, xf)
    return xf.reshape(b, s, d)


def kernel(x_prompt, x_sample, norm_ab, w_in_ab, w_out_ab, a_v_norm, a_w_s, a_b_s, b_q_norm, b_k_norm,
           b_sink, norm_c, w_in_c, w_out_c, c_q_norm, c_k_norm):
    s_max = max(x_prompt.shape[1], x_sample.shape[1])
    pos = jnp.arange(s_max, dtype=F32)
    row = jnp.floor(pos / GRID_W)
    col = pos - row * GRID_W
    tables = (
        _rope_tables([(pos, 0)], ROPE_THETA, ROPE_DIMS // 2, s_max),
        _rope_tables([(row, 0), (col, HEAD_DIM // 2)], AXIAL_THETA, HEAD_DIM // 4, s_max),
    )

    c0 = 3 * A_WIDTH + B_WIDTH
    c1 = c0 + 2 * B_KV_HEADS * HEAD_DIM
    even_w = []
    for i in range(norm_ab.shape[0]):
        w = w_in_ab[i]
        w_perm = jnp.concatenate([w[:, :c0], w[:, c1:], w[:, c0:c1]], axis=1).astype(BF16)
        even_w.append((
            norm_ab[i][None, :], w_perm, w_out_ab[i].astype(BF16),
            a_v_norm[i][None, :], a_w_s[i].astype(BF16),
            jnp.broadcast_to(a_b_s[i][:, :, None], (A_GROUPS, BLOCK, HEAD_DIM)),
            b_q_norm[i][None, :], b_k_norm[i][None, :], b_sink[i],
        ))
    odd_w = []
    for i in range(norm_c.shape[0]):
        odd_w.append((norm_c[i][None, :], w_in_c[i].astype(BF16), w_out_c[i].astype(BF16),
                      c_q_norm[i][None, :], c_k_norm[i][None, :]))

    y_prompt = _trunk(x_prompt, tables, even_w, odd_w)
    y_sample = _trunk(x_sample, tables, even_w, odd_w)
    return (y_prompt, y_sample)
```

```python
import functools
import math

import jax
import jax.numpy as jnp
from jax import lax
from jax.experimental import pallas as pl
from jax.experimental.pallas import tpu as pltpu

D_MODEL = 2048
HEAD_DIM = 128
BLOCK = 128
GRID_W = 64
EPS = 1e-6
A_WIDTH = D_MODEL // 2
A_GROUPS = A_WIDTH // HEAD_DIM
B_WIDTH = D_MODEL // 2
B_HEADS = B_WIDTH // HEAD_DIM
B_KV_HEADS = 2
B_REP = B_HEADS // B_KV_HEADS
ROPE_THETA = 500000.0
ROPE_DIMS = HEAD_DIM // 4
C_WIDTH = D_MODEL
C_HEADS = C_WIDTH // HEAD_DIM
C_KV_HEADS = 4
C_REP = C_HEADS // C_KV_HEADS
AXIAL_THETA = 10000.0
AB_IN = 3 * A_WIDTH + 2 * B_WIDTH + 2 * B_KV_HEADS * HEAD_DIM
C_IN = 2 * C_WIDTH + 2 * C_KV_HEADS * HEAD_DIM
NEG_BIG = -1e30
LOG2_E = 1.4426950408889634
Q_SCALE = HEAD_DIM ** -0.5 * LOG2_E

VMEM_LIMIT_BYTES = 56 * 1024 * 1024
PROJ_TN = 512

F32 = jnp.float32
BF16 = jnp.bfloat16

_AXIAL_SEGMENTS = ((0, 32), (64, 96), (32, 64), (96, 128))
_PARTIAL_SEGMENTS = ((0, 16), (32, 80), (16, 32), (80, 128))

_EVEN_U, _EVEN_V, _EVEN_GA, _EVEN_Q, _EVEN_GB = 0, 1, 2, 3, 4
_EVEN_K = (4 * A_WIDTH + B_WIDTH) // (B_KV_HEADS * HEAD_DIM)
_EVEN_VB = _EVEN_K + 1
_EVEN_KINDS = (
    ("gelu",) * 4, ("gelu",) * 4,
    ("gelu_norm",) * 4, ("gelu_norm",) * 4,
    ("silu",) * 4, ("silu",) * 4,
    ("norm_rope",) * 4, ("norm_rope",) * 4,
    ("silu",) * 4, ("silu",) * 4,
    ("norm_rope", "norm_rope", "none", "none"),
)
_ODD_K_BLK128 = C_WIDTH // HEAD_DIM
_ODD_V_BLK128 = (C_WIDTH + C_KV_HEADS * HEAD_DIM) // HEAD_DIM
_ODD_G_BLK512 = (C_WIDTH + 2 * C_KV_HEADS * HEAD_DIM) // 512
_ODD_KINDS = (("norm_rope",) * 4,) * 5 + (("none",) * 4,) + (("silu",) * 4,) * 4


def _params(*sem):
    return pltpu.CompilerParams(dimension_semantics=sem, vmem_limit_bytes=VMEM_LIMIT_BYTES)


def _gelu(x):
    return 0.5 * x * (1.0 + jnp.tanh(0.7978845608028654 * (x + 0.044715 * (x * x * x))))


def _silu(x):
    return x / (1.0 + jnp.exp(-x))


def _head_norm(x, gain):
    ms = jnp.mean(x * x, axis=-1, keepdims=True)
    return x * lax.rsqrt(ms + EPS) * gain


def _epilogue(kind, x, gain, cos_ref, sin_ref):
    if kind == "none":
        return x
    if kind == "gelu":
        return _gelu(x)
    if kind == "silu":
        return _silu(x)
    if kind == "gelu_norm":
        return _head_norm(_gelu(x), gain)
    assert kind == "norm_rope"
    xn = _head_norm(x, gain)
    return xn * cos_ref[...] + pltpu.roll(xn, HEAD_DIM // 2, axis=1) * sin_ref[...]


def _in_proj_kernel(x_ref, g_ref, w_ref, cg_ref, cos_ref, sin_ref, o_ref, h_ref, *, block_kinds):
    x = x_ref[...]
    ms = jnp.mean(x * x, axis=-1, keepdims=True)
    h_ref[...] = (x * lax.rsqrt(ms + EPS) * g_ref[...]).astype(BF16)
    for blk, kinds in enumerate(block_kinds):
        base = blk * PROJ_TN
        acc = jnp.dot(h_ref[...], w_ref[:, base:base + PROJ_TN], preferred_element_type=F32)
        for c, kind in enumerate(kinds):
            sl = slice(base + c * HEAD_DIM, base + (c + 1) * HEAD_DIM)
            o_ref[:, sl] = _epilogue(kind, acc[:, c * HEAD_DIM:(c + 1) * HEAD_DIM], cg_ref[:, sl],
                                     cos_ref, sin_ref).astype(o_ref.dtype)


def _in_proj(x, gain, w, col_gain, cos, sin, block_kinds, *, seq, tm=512):
    t, d = x.shape
    n = w.shape[1]
    tm = min(tm, seq)
    assert t % tm == 0 and seq % tm == 0 and n == PROJ_TN * len(block_kinds)
    tab_blocks = seq // tm
    return pl.pallas_call(
        functools.partial(_in_proj_kernel, block_kinds=block_kinds),
        out_shape=jax.ShapeDtypeStruct((t, n), BF16),
        grid=(t // tm,),
        in_specs=[
            pl.BlockSpec((tm, d), lambda i: (i, 0)),
            pl.BlockSpec((1, d), lambda i: (0, 0)),
            pl.BlockSpec((d, n), lambda i: (0, 0), pipeline_mode=pl.Buffered(1)),
            pl.BlockSpec((1, n), lambda i: (0, 0)),
            pl.BlockSpec((tm, HEAD_DIM), lambda i: (i % tab_blocks, 0)),
            pl.BlockSpec((tm, HEAD_DIM), lambda i: (i % tab_blocks, 0)),
        ],
        out_specs=pl.BlockSpec((tm, n), lambda i: (i, 0)),
        scratch_shapes=[pltpu.VMEM((tm, d), BF16)],
        compiler_params=_params("parallel"),
        name="in_proj",
    )(x, gain, w, col_gain, cos, sin)


def _matmul_residual_kernel(y_ref, w_ref, x_ref, o_ref):
    o_ref[...] = x_ref[...] + jnp.dot(y_ref[...], w_ref[...], preferred_element_type=F32)


def _matmul_residual(y, w, x, *, tm=512):
    t, k = y.shape
    n = w.shape[1]
    tm = min(tm, t)
    assert t % tm == 0
    return pl.pallas_call(
        _matmul_residual_kernel,
        out_shape=jax.ShapeDtypeStruct((t, n), F32),
        grid=(t // tm,),
        in_specs=[
            pl.BlockSpec((tm, k), lambda i: (i, 0)),
            pl.BlockSpec((k, n), lambda i: (0, 0)),
            pl.BlockSpec((tm, n), lambda i: (i, 0)),
        ],
        out_specs=pl.BlockSpec((tm, n), lambda i: (i, 0)),
        compiler_params=_params("parallel"),
        name="matmul_residual",
    )(y, w, x)


def _even_mixer_kernel(u_ref, vn_ref, ga_ref, q_ref, gb_ref,
                       kp_ref, kc_ref, kn_ref, vp_ref, vc_ref, vnx_ref,
                       ws_ref, bs_ref, sink_ref, o_ref, *, nb):
    t = pl.program_id(1)

    for g in range(A_GROUPS):
        sl = slice(g * HEAD_DIM, (g + 1) * HEAD_DIM)
        mixed = jnp.dot(ws_ref[g], vn_ref[:, sl], preferred_element_type=F32) + bs_ref[g]
        o_ref[:, sl] = (u_ref[:, sl].astype(F32) * mixed * ga_ref[:, sl].astype(F32)).astype(o_ref.dtype)

    k_win = jnp.concatenate([kp_ref[...], kc_ref[...], kn_ref[...]], axis=0)
    v_win = jnp.concatenate([vp_ref[...], vc_ref[...], vnx_ref[...]], axis=0)

    qi = lax.broadcasted_iota(jnp.int32, (BLOCK, 3 * BLOCK), 0)
    kj = lax.broadcasted_iota(jnp.int32, (BLOCK, 3 * BLOCK), 1)
    band = (kj - qi >= 0) & (kj - qi <= 2 * BLOCK)
    first_key = jnp.where(t > 0, 0, BLOCK)
    end_key = jnp.where(t < nb - 1, 3 * BLOCK, 2 * BLOCK)
    mask = band & (kj >= first_key) & (kj < end_key)

    for h in range(B_KV_HEADS):
        ksl = slice(h * HEAD_DIM, (h + 1) * HEAD_DIM)
        q_stack = jnp.concatenate(
            [q_ref[:, (h * B_REP + r) * HEAD_DIM:(h * B_REP + r + 1) * HEAD_DIM] for r in range(B_REP)], axis=0)
        s = lax.dot_general(q_stack, k_win[:, ksl], (((1,), (1,)), ((), ())),
                            preferred_element_type=F32)
        p_rows, inv_rows = [], []
        for r in range(B_REP):
            sink = sink_ref[h * B_REP + r] * LOG2_E
            s_r = jnp.where(mask, s[r * BLOCK:(r + 1) * BLOCK, :], NEG_BIG)
            m = jnp.maximum(jnp.max(s_r, axis=1, keepdims=True), sink)
            p = jnp.exp2(s_r - m)
            denom = jnp.sum(p, axis=1, keepdims=True) + jnp.exp2(sink - m)
            p_rows.append(p.astype(BF16))
            inv_rows.append(1.0 / denom)
        p_stack = jnp.concatenate(p_rows, axis=0)
        o = jnp.dot(p_stack, v_win[:, ksl], preferred_element_type=F32)
        for r in range(B_REP):
            col = (h * B_REP + r) * HEAD_DIM
            o_r = o[r * BLOCK:(r + 1) * BLOCK, :] * inv_rows[r]
            o_ref[:, A_WIDTH + col:A_WIDTH + col + HEAD_DIM] = (
                o_r * gb_ref[:, col:col + HEAD_DIM].astype(F32)).astype(o_ref.dtype)


def _even_mixer(z, a_ws, a_bs, sink):
    b, s, _ = z.shape
    nb = s // BLOCK
    wide = lambda c: pl.BlockSpec((None, BLOCK, A_WIDTH), lambda bi, ti: (bi, ti, c))
    kv_w = B_KV_HEADS * HEAD_DIM
    prev = lambda ti: jnp.maximum(ti - 1, 0)
    nxt = lambda ti: jnp.minimum(ti + 1, nb - 1)
    same = lambda ti: ti
    kv = lambda c, f: pl.BlockSpec((None, BLOCK, kv_w), lambda bi, ti: (bi, f(ti), c))
    full = lambda shape: pl.BlockSpec(shape, lambda bi, ti: (0,) * len(shape))
    in_specs = [
        wide(_EVEN_U), wide(_EVEN_V), wide(_EVEN_GA), wide(_EVEN_Q), wide(_EVEN_GB),
        kv(_EVEN_K, prev), kv(_EVEN_K, same), kv(_EVEN_K, nxt),
        kv(_EVEN_VB, prev), kv(_EVEN_VB, same), kv(_EVEN_VB, nxt),
        full((A_GROUPS, BLOCK, BLOCK)), full((A_GROUPS, BLOCK, HEAD_DIM)),
        pl.BlockSpec(memory_space=pltpu.SMEM),
    ]
    return pl.pallas_call(
        functools.partial(_even_mixer_kernel, nb=nb),
        out_shape=jax.ShapeDtypeStruct((b, s, A_WIDTH + B_WIDTH), BF16),
        grid=(b, nb),
        in_specs=in_specs,
        out_specs=pl.BlockSpec((None, BLOCK, A_WIDTH + B_WIDTH), lambda bi, ti: (bi, ti, 0)),
        compiler_params=_params("parallel", "parallel"),
        name="even_mixer",
    )(z, z, z, z, z, z, z, z, z, z, z, a_ws, a_bs, sink)


def _flash_kernel(q_ref, k_ref, v_ref, g_ref, o_ref, qs_ref, s_ref, m_ref, acc_ref, *, tq, tk, n_kv):
    for r in range(C_REP):
        qs_ref[r * tq:(r + 1) * tq, :] = q_ref[:, r * HEAD_DIM:(r + 1) * HEAD_DIM]
    m_ref[...] = jnp.full(m_ref.shape, NEG_BIG, F32)
    acc_ref[...] = jnp.zeros(acc_ref.shape, F32)
    n_lane_tiles = tk // HEAD_DIM
    ones = jnp.ones((tk, HEAD_DIM), BF16)

    def scores(j):
        start = pl.multiple_of(j * tk, tk)
        return lax.dot_general(qs_ref[...], k_ref[pl.ds(start, tk), :], (((1,), (1,)), ((), ())),
                               preferred_element_type=F32)

    def softmax_pv(slot, j):
        start = pl.multiple_of(j * tk, tk)
        v_ext = jnp.concatenate([v_ref[pl.ds(start, tk), :], ones], axis=1)
        tiles = [s_ref[slot, :, c * HEAD_DIM:(c + 1) * HEAD_DIM] for c in range(n_lane_tiles)]
        m_prev = m_ref[...]
        m_new = jnp.maximum(m_prev, jnp.max(functools.reduce(jnp.maximum, tiles), axis=1, keepdims=True))
        alpha = jnp.exp2(m_prev - m_new)
        p = jnp.concatenate([jnp.exp2(tile - m_new).astype(BF16) for tile in tiles], axis=1)
        pv = jnp.dot(p, v_ext, preferred_element_type=F32)
        acc_ref[...] = jnp.concatenate([alpha, alpha], axis=1) * acc_ref[...] + pv
        m_ref[...] = m_new

    s_ref[0] = scores(0)

    def body(jj, carry):
        j0 = 2 * jj
        s_ref[1] = scores(j0 + 1)
        softmax_pv(0, j0)
        s_ref[0] = scores(j0 + 2)
        softmax_pv(1, j0 + 1)
        return carry

    lax.fori_loop(0, n_kv // 2 - 1, body, 0)
    s_ref[1] = scores(n_kv - 1)
    softmax_pv(0, n_kv - 2)
    softmax_pv(1, n_kv - 1)

    for r in range(C_REP):
        sl = slice(r * HEAD_DIM, (r + 1) * HEAD_DIM)
        rows = slice(r * tq, (r + 1) * tq)
        o_r = acc_ref[rows, :HEAD_DIM] / acc_ref[rows, HEAD_DIM:]
        o_ref[:, sl] = (o_r * g_ref[:, sl].astype(F32)).astype(o_ref.dtype)


def _flash(z, *, tq=512, tk=512):
    b, s, _ = z.shape
    tq, tk = min(tq, s), min(tk, s)
    n_kv = s // tk
    assert s % tq == 0 and s % tk == 0 and n_kv % 2 == 0
    qw = C_REP * HEAD_DIM
    rows = C_REP * tq
    return pl.pallas_call(
        functools.partial(_flash_kernel, tq=tq, tk=tk, n_kv=n_kv),
        out_shape=jax.ShapeDtypeStruct((b, s, C_WIDTH), BF16),
        grid=(b, C_KV_HEADS, s // tq),
        in_specs=[
            pl.BlockSpec((None, tq, qw), lambda bi, hi, qi: (bi, qi, hi)),
            pl.BlockSpec((None, s, HEAD_DIM), lambda bi, hi, qi: (bi, 0, _ODD_K_BLK128 + hi)),
            pl.BlockSpec((None, s, HEAD_DIM), lambda bi, hi, qi: (bi, 0, _ODD_V_BLK128 + hi)),
            pl.BlockSpec((None, tq, qw), lambda bi, hi, qi: (bi, qi, _ODD_G_BLK512 + hi)),
        ],
        out_specs=pl.BlockSpec((None, tq, qw), lambda bi, hi, qi: (bi, qi, hi)),
        scratch_shapes=[
            pltpu.VMEM((rows, HEAD_DIM), BF16),
            pltpu.VMEM((2, rows, tk), F32),
            pltpu.VMEM((rows, HEAD_DIM), F32),
            pltpu.VMEM((rows, 2 * HEAD_DIM), F32),
        ],
        compiler_params=_params("parallel", "parallel", "arbitrary"),
        name="flash_gqa",
    )(z, z, z, z)


def _reorder_heads(a, segments):
    lead = a.shape[:-1]
    x = a.reshape(lead + (a.shape[-1] // HEAD_DIM, HEAD_DIM))
    x = jnp.concatenate([x[..., lo:hi] for lo, hi in segments], axis=-1)
    return x.reshape(a.shape)


def _rope_tables(pos_list, theta, half, s):
    inv = jnp.power(F32(theta), -jnp.arange(half, dtype=F32) * (1.0 / half))
    cos_parts, sin_parts = [], []
    for pos in pos_list:
        ang = pos[:, None] * inv[None, :]
        cos_parts.append(jnp.cos(ang))
        sin_parts.append(jnp.sin(ang))
    pad = HEAD_DIM // 2 - half * len(pos_list)
    cos_half = jnp.concatenate(cos_parts + [jnp.ones((s, pad), F32)], axis=1)
    sin_half = jnp.concatenate(sin_parts + [jnp.zeros((s, pad), F32)], axis=1)
    return (jnp.concatenate([cos_half, cos_half], axis=1),
            jnp.concatenate([-sin_half, sin_half], axis=1))


def _trunk(x, tables, even_w, odd_w):
    b, s, d = x.shape
    (cos_b, sin_b), (cos_c, sin_c) = tables
    cos_b, sin_b, cos_c, sin_c = cos_b[:s], sin_b[:s], cos_c[:s], sin_c[:s]
    xf = x.reshape(b * s, d)
    for layer in range(len(even_w) + len(odd_w)):
        i = layer // 2
        if layer % 2 == 0:
            norm_g, w_in, col_gain, w_out, a_ws, a_bs, sink = even_w[i]
            z = _in_proj(xf, norm_g, w_in, col_gain, cos_b, sin_b, _EVEN_KINDS, seq=s).reshape(b, s, AB_IN)
            y = _even_mixer(z, a_ws, a_bs, sink)
        else:
            norm_g, w_in, col_gain, w_out = odd_w[i]
            z = _in_proj(xf, norm_g, w_in, col_gain, cos_c, sin_c, _ODD_KINDS, seq=s).reshape(b, s, C_IN)
            y = _flash(z)
        xf = _matmul_residual(y.reshape(b * s, d), w_out, xf)
    return xf.reshape(b, s, d)


def kernel(x_prompt, x_sample, norm_ab, w_in_ab, w_out_ab, a_v_norm, a_w_s, a_b_s, b_q_norm, b_k_norm,
           b_sink, norm_c, w_in_c, w_out_c, c_q_norm, c_k_norm):
    s_max = max(x_prompt.shape[1], x_sample.shape[1])
    pos = jnp.arange(s_max, dtype=F32)
    row = jnp.floor(pos / GRID_W)
    col = pos - row * GRID_W
    tables = (
        _rope_tables([pos], ROPE_THETA, ROPE_DIMS // 2, s_max),
        _rope_tables([row, col], AXIAL_THETA, HEAD_DIM // 4, s_max),
    )

    kv_b = B_KV_HEADS * HEAD_DIM
    kv_c = C_KV_HEADS * HEAD_DIM
    c0 = 3 * A_WIDTH
    c1 = c0 + B_WIDTH
    c2 = c1 + kv_b
    c3 = c2 + kv_b
    even_w = []
    for i in range(norm_ab.shape[0]):
        w = w_in_ab[i]
        w_cols = jnp.concatenate([
            w[:, :c0], _reorder_heads(w[:, c0:c1], _PARTIAL_SEGMENTS), w[:, c3:],
            _reorder_heads(w[:, c1:c2], _PARTIAL_SEGMENTS), w[:, c2:c3]], axis=1).astype(BF16)
        q_gain = _reorder_heads(b_q_norm[i], _PARTIAL_SEGMENTS) * Q_SCALE
        k_gain = _reorder_heads(b_k_norm[i], _PARTIAL_SEGMENTS)
        col_gain = jnp.concatenate([
            jnp.ones((A_WIDTH,), F32), a_v_norm[i], jnp.ones((A_WIDTH,), F32),
            jnp.tile(q_gain, B_HEADS), jnp.ones((B_WIDTH,), F32),
            jnp.tile(k_gain, B_KV_HEADS), jnp.ones((kv_b,), F32)])[None, :]
        even_w.append((
            norm_ab[i][None, :], w_cols, col_gain, w_out_ab[i].astype(BF16), a_w_s[i].astype(BF16),
            jnp.broadcast_to(a_b_s[i][:, :, None], (A_GROUPS, BLOCK, HEAD_DIM)), b_sink[i],
        ))
    odd_w = []
    for i in range(norm_c.shape[0]):
        w = w_in_c[i]
        qk_end = C_WIDTH + kv_c
        w_cols = jnp.concatenate([_reorder_heads(w[:, :qk_end], _AXIAL_SEGMENTS), w[:, qk_end:]],
                                 axis=1).astype(BF16)
        q_gain = _reorder_heads(c_q_norm[i], _AXIAL_SEGMENTS) * Q_SCALE
        k_gain = _reorder_heads(c_k_norm[i], _AXIAL_SEGMENTS)
        col_gain = jnp.concatenate([
            jnp.tile(q_gain, C_HEADS), jnp.tile(k_gain, C_KV_HEADS),
            jnp.ones((kv_c + C_WIDTH,), F32)])[None, :]
        odd_w.append((norm_c[i][None, :], w_cols, col_gain, w_out_c[i].astype(BF16)))

    y_prompt = _trunk(x_prompt, tables, even_w, odd_w)
    y_sample = _trunk(x_sample, tables, even_w, odd_w)
    return (y_prompt, y_sample)
```

```python
import functools

import jax
import jax.numpy as jnp
from jax import lax
from jax.experimental import pallas as pl
from jax.experimental.pallas import tpu as pltpu

D_MODEL = 2048
HEAD_DIM = 128
BLOCK = 128
GRID_W = 64
EPS = 1e-6
A_WIDTH = D_MODEL // 2
A_GROUPS = A_WIDTH // HEAD_DIM
B_WIDTH = D_MODEL // 2
B_HEADS = B_WIDTH // HEAD_DIM
B_KV_HEADS = 2
B_REP = B_HEADS // B_KV_HEADS
ROPE_THETA = 500000.0
ROPE_DIMS = HEAD_DIM // 4
C_WIDTH = D_MODEL
C_HEADS = C_WIDTH // HEAD_DIM
C_KV_HEADS = 4
C_REP = C_HEADS // C_KV_HEADS
AXIAL_THETA = 10000.0
AB_IN = 3 * A_WIDTH + 2 * B_WIDTH + 2 * B_KV_HEADS * HEAD_DIM
C_IN = 2 * C_WIDTH + 2 * C_KV_HEADS * HEAD_DIM
NEG_BIG = -1e30
LOG2_E = 1.4426950408889634
Q_SCALE = HEAD_DIM ** -0.5 * LOG2_E

VMEM_LIMIT_BYTES = 56 * 1024 * 1024
PROJ_TN = 256

F32 = jnp.float32
BF16 = jnp.bfloat16

_AXIAL_SEGMENTS = ((0, 32), (64, 96), (32, 64), (96, 128))
_PARTIAL_SEGMENTS = ((0, 16), (32, 80), (16, 32), (80, 128))

_EVEN_U, _EVEN_V, _EVEN_GA, _EVEN_Q, _EVEN_GB = 0, 1, 2, 3, 4
_EVEN_K = (4 * A_WIDTH + B_WIDTH) // (B_KV_HEADS * HEAD_DIM)
_EVEN_VB = _EVEN_K + 1
_EVEN_Q_GAIN = A_WIDTH
_EVEN_K_GAIN = A_WIDTH + HEAD_DIM
_EVEN_CHUNKS = ((("gelu", None),) * A_GROUPS
                + tuple(("gelu_norm", g * HEAD_DIM) for g in range(A_GROUPS))
                + (("silu", None),) * A_GROUPS
                + (("norm_rope", _EVEN_Q_GAIN),) * B_HEADS
                + (("silu", None),) * B_HEADS
                + (("norm_rope", _EVEN_K_GAIN),) * B_KV_HEADS
                + (("none", None),) * B_KV_HEADS)
_ODD_K_BLK128 = C_WIDTH // HEAD_DIM
_ODD_V_BLK128 = (C_WIDTH + C_KV_HEADS * HEAD_DIM) // HEAD_DIM
_ODD_G_BLK512 = (C_WIDTH + 2 * C_KV_HEADS * HEAD_DIM) // 512
_ODD_CHUNKS = ((("norm_rope", 0),) * C_HEADS + (("norm_rope", HEAD_DIM),) * C_KV_HEADS
               + (("none", None),) * C_KV_HEADS + (("silu", None),) * C_HEADS)


def _params(*sem):
    return pltpu.CompilerParams(dimension_semantics=sem, vmem_limit_bytes=VMEM_LIMIT_BYTES)


def _gelu(x):
    return 0.5 * x * (1.0 + jnp.tanh(0.7978845608028654 * (x + 0.044715 * (x * x * x))))


def _silu(x):
    return x / (1.0 + jnp.exp(-x))


def _head_norm(x, gain):
    ms = jnp.mean(x * x, axis=-1, keepdims=True)
    return x * lax.rsqrt(ms + EPS) * gain


def _epilogue(kind, x, gain, cos_ref, sin_ref):
    if kind == "none":
        return x
    if kind == "gelu":
        return _gelu(x)
    if kind == "silu":
        return _silu(x)
    if kind == "gelu_norm":
        return _head_norm(_gelu(x), gain)
    assert kind == "norm_rope"
    xn = _head_norm(x, gain)
    return xn * cos_ref[...] + pltpu.roll(xn, HEAD_DIM // 2, axis=1) * sin_ref[...]


def _in_proj_kernel(x_ref, g_ref, w_ref, gains_ref, cos_ref, sin_ref, o_ref, h_ref, *, chunks):
    x = x_ref[...]
    ms = jnp.mean(x * x, axis=-1, keepdims=True)
    h_ref[...] = (x * lax.rsqrt(ms + EPS) * g_ref[...]).astype(BF16)
    chunks_per_dot = PROJ_TN // HEAD_DIM
    cost = {"norm_rope": 0, "gelu_norm": 1, "none": 2, "gelu": 3, "silu": 4}
    starts = sorted(range(0, len(chunks), chunks_per_dot), key=lambda f: (cost[chunks[f][0]], f))
    for first in starts:
        base = first * HEAD_DIM
        acc = jnp.dot(h_ref[...], w_ref[:, base:base + PROJ_TN], preferred_element_type=F32)
        for c, (kind, gain_at) in enumerate(chunks[first:first + chunks_per_dot]):
            sl = slice(base + c * HEAD_DIM, base + (c + 1) * HEAD_DIM)
            gain = None if gain_at is None else gains_ref[:, gain_at:gain_at + HEAD_DIM]
            o_ref[:, sl] = _epilogue(kind, acc[:, c * HEAD_DIM:(c + 1) * HEAD_DIM], gain,
                                     cos_ref, sin_ref).astype(o_ref.dtype)


def _in_proj(x, gain, w, gains, cos, sin, chunks, *, seq, tm=512):
    t, d = x.shape
    n = w.shape[1]
    tm = min(tm, seq)
    assert t % tm == 0 and seq % tm == 0 and n == HEAD_DIM * len(chunks) and n % PROJ_TN == 0
    tab_blocks = seq // tm
    return pl.pallas_call(
        functools.partial(_in_proj_kernel, chunks=chunks),
        out_shape=jax.ShapeDtypeStruct((t, n), BF16),
        grid=(t // tm,),
        in_specs=[
            pl.BlockSpec((tm, d), lambda i: (i, 0)),
            pl.BlockSpec((1, d), lambda i: (0, 0)),
            pl.BlockSpec((d, n), lambda i: (0, 0), pipeline_mode=pl.Buffered(1)),
            pl.BlockSpec((1, gains.shape[1]), lambda i: (0, 0)),
            pl.BlockSpec((tm, HEAD_DIM), lambda i: (i % tab_blocks, 0)),
            pl.BlockSpec((tm, HEAD_DIM), lambda i: (i % tab_blocks, 0)),
        ],
        out_specs=pl.BlockSpec((tm, n), lambda i: (i, 0)),
        scratch_shapes=[pltpu.VMEM((tm, d), BF16)],
        compiler_params=_params("parallel"),
        name="in_proj",
    )(x, gain, w, gains, cos, sin)


def _matmul_residual_kernel(y_ref, w_ref, x_ref, o_ref):
    o_ref[...] = x_ref[...] + jnp.dot(y_ref[...], w_ref[...], preferred_element_type=F32)


def _matmul_residual(y, w, x, *, tm=512):
    t, k = y.shape
    n = w.shape[1]
    tm = min(tm, t)
    assert t % tm == 0
    return pl.pallas_call(
        _matmul_residual_kernel,
        out_shape=jax.ShapeDtypeStruct((t, n), F32),
        grid=(t // tm,),
        in_specs=[
            pl.BlockSpec((tm, k), lambda i: (i, 0)),
            pl.BlockSpec((k, n), lambda i: (0, 0)),
            pl.BlockSpec((tm, n), lambda i: (i, 0)),
        ],
        out_specs=pl.BlockSpec((tm, n), lambda i: (i, 0)),
        compiler_params=_params("parallel"),
        name="matmul_residual",
    )(y, w, x)


def _even_mixer_kernel(u_ref, vn_ref, ga_ref, q_ref, gb_ref,
                       kp_ref, kc_ref, kn_ref, vp_ref, vc_ref, vnx_ref,
                       ws_ref, bs_ref, sink_ref, o_ref, *, nb):
    t = pl.program_id(1)

    for g in range(A_GROUPS):
        sl = slice(g * HEAD_DIM, (g + 1) * HEAD_DIM)
        mixed = jnp.dot(ws_ref[g], vn_ref[:, sl], preferred_element_type=F32) + bs_ref[g]
        o_ref[:, sl] = (u_ref[:, sl].astype(F32) * mixed * ga_ref[:, sl].astype(F32)).astype(o_ref.dtype)

    k_win = jnp.concatenate([kp_ref[...], kc_ref[...], kn_ref[...]], axis=0)
    v_win = jnp.concatenate([vp_ref[...], vc_ref[...], vnx_ref[...]], axis=0)

    qi = lax.broadcasted_iota(jnp.int32, (BLOCK, 3 * BLOCK), 0)
    kj = lax.broadcasted_iota(jnp.int32, (BLOCK, 3 * BLOCK), 1)
    band = (kj - qi >= 0) & (kj - qi <= 2 * BLOCK)
    first_key = jnp.where(t > 0, 0, BLOCK)
    end_key = jnp.where(t < nb - 1, 3 * BLOCK, 2 * BLOCK)
    mask = band & (kj >= first_key) & (kj < end_key)

    for h in range(B_KV_HEADS):
        ksl = slice(h * HEAD_DIM, (h + 1) * HEAD_DIM)
        q_stack = jnp.concatenate(
            [q_ref[:, (h * B_REP + r) * HEAD_DIM:(h * B_REP + r + 1) * HEAD_DIM] for r in range(B_REP)], axis=0)
        s = lax.dot_general(q_stack, k_win[:, ksl], (((1,), (1,)), ((), ())),
                            preferred_element_type=F32)
        p_rows, inv_rows = [], []
        for r in range(B_REP):
            sink = sink_ref[h * B_REP + r] * LOG2_E
            s_r = jnp.where(mask, s[r * BLOCK:(r + 1) * BLOCK, :], NEG_BIG)
            m = jnp.maximum(jnp.max(s_r, axis=1, keepdims=True), sink)
            p = jnp.exp2(s_r - m)
            denom = jnp.sum(p, axis=1, keepdims=True) + jnp.exp2(sink - m)
            p_rows.append(p.astype(BF16))
            inv_rows.append(1.0 / denom)
        p_stack = jnp.concatenate(p_rows, axis=0)
        o = jnp.dot(p_stack, v_win[:, ksl], preferred_element_type=F32)
        for r in range(B_REP):
            col = (h * B_REP + r) * HEAD_DIM
            o_r = o[r * BLOCK:(r + 1) * BLOCK, :] * inv_rows[r]
            o_ref[:, A_WIDTH + col:A_WIDTH + col + HEAD_DIM] = (
                o_r * gb_ref[:, col:col + HEAD_DIM].astype(F32)).astype(o_ref.dtype)


def _even_mixer(z, a_ws, a_bs, sink):
    b, s, _ = z.shape
    nb = s // BLOCK
    wide = lambda c: pl.BlockSpec((None, BLOCK, A_WIDTH), lambda bi, ti: (bi, ti, c))
    kv_w = B_KV_HEADS * HEAD_DIM
    prev = lambda ti: jnp.maximum(ti - 1, 0)
    nxt = lambda ti: jnp.minimum(ti + 1, nb - 1)
    same = lambda ti: ti
    kv = lambda c, f: pl.BlockSpec((None, BLOCK, kv_w), lambda bi, ti: (bi, f(ti), c))
    full = lambda shape: pl.BlockSpec(shape, lambda bi, ti: (0,) * len(shape))
    in_specs = [
        wide(_EVEN_U), wide(_EVEN_V), wide(_EVEN_GA), wide(_EVEN_Q), wide(_EVEN_GB),
        kv(_EVEN_K, prev), kv(_EVEN_K, same), kv(_EVEN_K, nxt),
        kv(_EVEN_VB, prev), kv(_EVEN_VB, same), kv(_EVEN_VB, nxt),
        full((A_GROUPS, BLOCK, BLOCK)), full((A_GROUPS, BLOCK, HEAD_DIM)),
        pl.BlockSpec(memory_space=pltpu.SMEM),
    ]
    return pl.pallas_call(
        functools.partial(_even_mixer_kernel, nb=nb),
        out_shape=jax.ShapeDtypeStruct((b, s, A_WIDTH + B_WIDTH), BF16),
        grid=(b, nb),
        in_specs=in_specs,
        out_specs=pl.BlockSpec((None, BLOCK, A_WIDTH + B_WIDTH), lambda bi, ti: (bi, ti, 0)),
        compiler_params=_params("parallel", "parallel"),
        name="even_mixer",
    )(z, z, z, z, z, z, z, z, z, z, z, a_ws, a_bs, sink)


def _flash_kernel(q_ref, q_next_ref, k_ref, v_ref, g_ref, o_ref,
                  qs_ref, qs_next_ref, s_ref, m_ref, acc_ref, *, tq, tk, n_kv):
    qi = pl.program_id(2)
    n_lane_tiles = tk // HEAD_DIM
    ones = jnp.ones((tk, HEAD_DIM), BF16)

    def stack_heads(src_ref, dst_ref):
        for r in range(C_REP):
            dst_ref[r * tq:(r + 1) * tq, :] = src_ref[:, r * HEAD_DIM:(r + 1) * HEAD_DIM]

    def scores(stacked_ref, j):
        start = pl.multiple_of(j * tk, tk)
        return lax.dot_general(stacked_ref[...], k_ref[pl.ds(start, tk), :], (((1,), (1,)), ((), ())),
                               preferred_element_type=F32)

    @pl.when(qi == 0)
    def _():
        stack_heads(q_ref, qs_ref)
        s_ref[0] = scores(qs_ref, 0)

    @pl.when(qi != 0)
    def _():
        qs_ref[...] = qs_next_ref[...]

    m_ref[...] = jnp.full(m_ref.shape, NEG_BIG, F32)
    acc_ref[...] = jnp.zeros(acc_ref.shape, F32)

    def softmax_pv(slot, j):
        start = pl.multiple_of(j * tk, tk)
        v_ext = jnp.concatenate([v_ref[pl.ds(start, tk), :], ones], axis=1)
        tiles = [s_ref[slot, :, c * HEAD_DIM:(c + 1) * HEAD_DIM] for c in range(n_lane_tiles)]
        m_prev = m_ref[...]
        m_new = jnp.maximum(m_prev, jnp.max(functools.reduce(jnp.maximum, tiles), axis=1, keepdims=True))
        alpha = jnp.exp2(m_prev - m_new)
        p = jnp.concatenate([jnp.exp2(tile - m_new).astype(BF16) for tile in tiles], axis=1)
        pv = jnp.dot(p, v_ext, preferred_element_type=F32)
        acc_ref[...] = jnp.concatenate([alpha, alpha], axis=1) * acc_ref[...] + pv
        m_ref[...] = m_new

    def run_pair(j0, last):
        s_ref[1] = scores(qs_ref, j0 + 1)
        softmax_pv(0, j0)
        if last:
            stack_heads(q_next_ref, qs_next_ref)
            s_ref[0] = scores(qs_next_ref, 0)
        else:
            s_ref[0] = scores(qs_ref, j0 + 2)
        softmax_pv(1, j0 + 1)

    def body(jj, carry):
        run_pair(2 * jj, False)
        return carry

    lax.fori_loop(0, n_kv // 2 - 1, body, 0)
    run_pair(n_kv - 2, True)

    for r in range(C_REP):
        sl = slice(r * HEAD_DIM, (r + 1) * HEAD_DIM)
        rows = slice(r * tq, (r + 1) * tq)
        o_r = acc_ref[rows, :HEAD_DIM] / acc_ref[rows, HEAD_DIM:]
        o_ref[:, sl] = (o_r * g_ref[:, sl].astype(F32)).astype(o_ref.dtype)


def _flash(z, *, tq=512, tk=512):
    b, s, _ = z.shape
    tq, tk = min(tq, s), min(tk, s)
    n_kv = s // tk
    assert s % tq == 0 and s % tk == 0 and n_kv % 2 == 0
    qw = C_REP * HEAD_DIM
    rows = C_REP * tq
    nq = s // tq
    return pl.pallas_call(
        functools.partial(_flash_kernel, tq=tq, tk=tk, n_kv=n_kv),
        out_shape=jax.ShapeDtypeStruct((b, s, C_WIDTH), BF16),
        grid=(b, C_KV_HEADS, nq),
        in_specs=[
            pl.BlockSpec((None, tq, qw), lambda bi, hi, qi: (bi, qi, hi)),
            pl.BlockSpec((None, tq, qw), lambda bi, hi, qi: (bi, jnp.minimum(qi + 1, nq - 1), hi)),
            pl.BlockSpec((None, s, HEAD_DIM), lambda bi, hi, qi: (bi, 0, _ODD_K_BLK128 + hi)),
            pl.BlockSpec((None, s, HEAD_DIM), lambda bi, hi, qi: (bi, 0, _ODD_V_BLK128 + hi)),
            pl.BlockSpec((None, tq, qw), lambda bi, hi, qi: (bi, qi, _ODD_G_BLK512 + hi)),
        ],
        out_specs=pl.BlockSpec((None, tq, qw), lambda bi, hi, qi: (bi, qi, hi)),
        scratch_shapes=[
            pltpu.VMEM((rows, HEAD_DIM), BF16),
            pltpu.VMEM((rows, HEAD_DIM), BF16),
            pltpu.VMEM((2, rows, tk), F32),
            pltpu.VMEM((rows, HEAD_DIM), F32),
            pltpu.VMEM((rows, 2 * HEAD_DIM), F32),
        ],
        compiler_params=_params("parallel", "parallel", "arbitrary"),
        name="flash_gqa",
    )(z, z, z, z, z)


def _reorder_heads(a, segments):
    lead = a.shape[:-1]
    x = a.reshape(lead + (a.shape[-1] // HEAD_DIM, HEAD_DIM))
    x = jnp.concatenate([x[..., lo:hi] for lo, hi in segments], axis=-1)
    return x.reshape(a.shape)


def _rope_tables(pos_list, theta, half, s):
    inv = jnp.power(F32(theta), -jnp.arange(half, dtype=F32) * (1.0 / half))
    cos_parts, sin_parts = [], []
    for pos in pos_list:
        ang = pos[:, None] * inv[None, :]
        cos_parts.append(jnp.cos(ang))
        sin_parts.append(jnp.sin(ang))
    pad = HEAD_DIM // 2 - half * len(pos_list)
    cos_half = jnp.concatenate(cos_parts + [jnp.ones((s, pad), F32)], axis=1)
    sin_half = jnp.concatenate(sin_parts + [jnp.zeros((s, pad), F32)], axis=1)
    return (jnp.concatenate([cos_half, cos_half], axis=1),
            jnp.concatenate([-sin_half, sin_half], axis=1))


def _trunk(x, tables, even_w, odd_w):
    b, s, d = x.shape
    (cos_b, sin_b), (cos_c, sin_c) = tables
    cos_b, sin_b, cos_c, sin_c = cos_b[:s], sin_b[:s], cos_c[:s], sin_c[:s]
    xf = x.reshape(b * s, d)
    for layer in range(len(even_w) + len(odd_w)):
        i = layer // 2
        if layer % 2 == 0:
            norm_g, w_in, gains, w_out, a_ws, a_bs, sink = even_w[i]
            z = _in_proj(xf, norm_g, w_in, gains, cos_b, sin_b, _EVEN_CHUNKS, seq=s).reshape(b, s, AB_IN)
            y = _even_mixer(z, a_ws, a_bs, sink)
        else:
            norm_g, w_in, gains, w_out = odd_w[i]
            z = _in_proj(xf, norm_g, w_in, gains, cos_c, sin_c, _ODD_CHUNKS, seq=s).reshape(b, s, C_IN)
            y = _flash(z)
        xf = _matmul_residual(y.reshape(b * s, d), w_out, xf)
    return xf.reshape(b, s, d)


def kernel(x_prompt, x_sample, norm_ab, w_in_ab, w_out_ab, a_v_norm, a_w_s, a_b_s, b_q_norm, b_k_norm,
           b_sink, norm_c, w_in_c, w_out_c, c_q_norm, c_k_norm):
    s_max = max(x_prompt.shape[1], x_sample.shape[1])
    pos = jnp.arange(s_max, dtype=F32)
    row = jnp.floor(pos / GRID_W)
    col = pos - row * GRID_W
    tables = (
        _rope_tables([pos], ROPE_THETA, ROPE_DIMS // 2, s_max),
        _rope_tables([row, col], AXIAL_THETA, HEAD_DIM // 4, s_max),
    )

    kv_b = B_KV_HEADS * HEAD_DIM
    kv_c = C_KV_HEADS * HEAD_DIM
    c0 = 3 * A_WIDTH
    c1 = c0 + B_WIDTH
    c2 = c1 + kv_b
    c3 = c2 + kv_b
    even_w = []
    for i in range(norm_ab.shape[0]):
        w = w_in_ab[i]
        w_cols = jnp.concatenate([
            w[:, :c0], _reorder_heads(w[:, c0:c1], _PARTIAL_SEGMENTS), w[:, c3:],
            _reorder_heads(w[:, c1:c2], _PARTIAL_SEGMENTS), w[:, c2:c3]], axis=1).astype(BF16)
        q_gain = _reorder_heads(b_q_norm[i], _PARTIAL_SEGMENTS) * Q_SCALE
        k_gain = _reorder_heads(b_k_norm[i], _PARTIAL_SEGMENTS)
        gains = jnp.concatenate([a_v_norm[i], q_gain, k_gain])[None, :]
        even_w.append((
            norm_ab[i][None, :], w_cols, gains, w_out_ab[i].astype(BF16), a_w_s[i].astype(BF16),
            jnp.broadcast_to(a_b_s[i][:, :, None], (A_GROUPS, BLOCK, HEAD_DIM)), b_sink[i],
        ))
    odd_w = []
    for i in range(norm_c.shape[0]):
        w = w_in_c[i]
        qk_end = C_WIDTH + kv_c
        w_cols = jnp.concatenate([_reorder_heads(w[:, :qk_end], _AXIAL_SEGMENTS), w[:, qk_end:]],
                                 axis=1).astype(BF16)
        q_gain = _reorder_heads(c_q_norm[i], _AXIAL_SEGMENTS) * Q_SCALE
        k_gain = _reorder_heads(c_k_norm[i], _AXIAL_SEGMENTS)
        gains = jnp.concatenate([q_gain, k_gain])[None, :]
        odd_w.append((norm_c[i][None, :], w_cols, gains, w_out_c[i].astype(BF16)))

    y_prompt = _trunk(x_prompt, tables, even_w, odd_w)
    y_sample = _trunk(x_sample, tables, even_w, odd_w)
    return (y_prompt, y_sample)
```

```python
import functools

import jax
import jax.numpy as jnp
import numpy as np
from jax import lax
from jax.experimental import pallas as pl
from jax.experimental.pallas import tpu as pltpu

D_MODEL = 2048
HEAD_DIM = 128
BLOCK = 128
GRID_W = 64
EPS = 1e-6
A_WIDTH = D_MODEL // 2
A_GROUPS = A_WIDTH // HEAD_DIM
B_WIDTH = D_MODEL // 2
B_HEADS = B_WIDTH // HEAD_DIM
B_KV_HEADS = 2
B_REP = B_HEADS // B_KV_HEADS
ROPE_THETA = 500000.0
ROPE_DIMS = HEAD_DIM // 4
C_WIDTH = D_MODEL
C_HEADS = C_WIDTH // HEAD_DIM
C_KV_HEADS = 4
C_REP = C_HEADS // C_KV_HEADS
AXIAL_THETA = 10000.0
AB_IN = 3 * A_WIDTH + 2 * B_WIDTH + 2 * B_KV_HEADS * HEAD_DIM
C_IN = 2 * C_WIDTH + 2 * C_KV_HEADS * HEAD_DIM
NEG_BIG = -1e30
LOG2_E = 1.4426950408889634
Q_SCALE = HEAD_DIM ** -0.5 * LOG2_E

VMEM_LIMIT_BYTES = 56 * 1024 * 1024
PROJ_TN = 256

F32 = jnp.float32
BF16 = jnp.bfloat16

_AXIAL_SEGMENTS = ((0, 32), (64, 96), (32, 64), (96, 128))
_PARTIAL_SEGMENTS = ((0, 16), (32, 80), (16, 32), (80, 128))

_EVEN_U, _EVEN_V, _EVEN_GA, _EVEN_Q, _EVEN_GB = 0, 1, 2, 3, 4
_EVEN_K = (4 * A_WIDTH + B_WIDTH) // (B_KV_HEADS * HEAD_DIM)
_EVEN_VB = _EVEN_K + 1
_EVEN_Q_GAIN = A_WIDTH
_EVEN_K_GAIN = A_WIDTH + HEAD_DIM
_EVEN_CHUNKS = ((("gelu", None),) * A_GROUPS
                + tuple(("gelu_norm", g * HEAD_DIM) for g in range(A_GROUPS))
                + (("silu", None),) * A_GROUPS
                + (("norm_rope", _EVEN_Q_GAIN),) * B_HEADS
                + (("silu", None),) * B_HEADS
                + (("norm_rope", _EVEN_K_GAIN),) * B_KV_HEADS
                + (("none", None),) * B_KV_HEADS)
_ODD_K_BLK128 = C_WIDTH // HEAD_DIM
_ODD_V_BLK128 = (C_WIDTH + C_KV_HEADS * HEAD_DIM) // HEAD_DIM
_ODD_G_BLK512 = (C_WIDTH + 2 * C_KV_HEADS * HEAD_DIM) // 512
_ODD_CHUNKS = ((("norm_rope", 0),) * C_HEADS + (("norm_rope", HEAD_DIM),) * C_KV_HEADS
               + (("none", None),) * C_KV_HEADS + (("silu", None),) * C_HEADS)


def _params(*sem):
    return pltpu.CompilerParams(dimension_semantics=sem, vmem_limit_bytes=VMEM_LIMIT_BYTES)


def _gelu(x):
    return 0.5 * x * (1.0 + jnp.tanh(0.7978845608028654 * (x + 0.044715 * (x * x * x))))


def _silu(x):
    return x / (1.0 + jnp.exp(-x))


def _head_norm(x, gain):
    ms = jnp.mean(x * x, axis=-1, keepdims=True)
    return x * lax.rsqrt(ms + EPS) * gain


def _epilogue(kind, x, gain, cos_ref, sin_ref):
    if kind == "none":
        return x
    if kind == "gelu":
        return _gelu(x)
    if kind == "silu":
        return _silu(x)
    if kind == "gelu_norm":
        return _head_norm(_gelu(x), gain)
    assert kind == "norm_rope"
    xn = _head_norm(x, gain)
    return xn * cos_ref[...] + pltpu.roll(xn, HEAD_DIM // 2, axis=1) * sin_ref[...]


def _in_proj_kernel(x_ref, g_ref, w_ref, gains_ref, cos_ref, sin_ref, o_ref, h_ref, *, chunks):
    x = x_ref[...]
    ms = jnp.mean(x * x, axis=-1, keepdims=True)
    h_ref[...] = (x * lax.rsqrt(ms + EPS) * g_ref[...]).astype(BF16)
    chunks_per_dot = PROJ_TN // HEAD_DIM
    cost = {"norm_rope": 0, "gelu_norm": 1, "none": 2, "gelu": 3, "silu": 4}
    starts = sorted(range(0, len(chunks), chunks_per_dot), key=lambda f: (cost[chunks[f][0]], f))
    for first in starts:
        base = first * HEAD_DIM
        acc = jnp.dot(h_ref[...], w_ref[:, base:base + PROJ_TN], preferred_element_type=F32)
        for c, (kind, gain_at) in enumerate(chunks[first:first + chunks_per_dot]):
            sl = slice(base + c * HEAD_DIM, base + (c + 1) * HEAD_DIM)
            gain = None if gain_at is None else gains_ref[:, gain_at:gain_at + HEAD_DIM]
            o_ref[:, sl] = _epilogue(kind, acc[:, c * HEAD_DIM:(c + 1) * HEAD_DIM], gain,
                                     cos_ref, sin_ref).astype(o_ref.dtype)


def _in_proj(x, gain, w, gains, cos, sin, chunks, *, seq, tm=512):
    t, d = x.shape
    n = w.shape[1]
    tm = min(tm, seq)
    assert t % tm == 0 and seq % tm == 0 and n == HEAD_DIM * len(chunks) and n % PROJ_TN == 0
    tab_blocks = seq // tm
    return pl.pallas_call(
        functools.partial(_in_proj_kernel, chunks=chunks),
        out_shape=jax.ShapeDtypeStruct((t, n), BF16),
        grid=(t // tm,),
        in_specs=[
            pl.BlockSpec((tm, d), lambda i: (i, 0)),
            pl.BlockSpec((1, d), lambda i: (0, 0)),
            pl.BlockSpec((d, n), lambda i: (0, 0), pipeline_mode=pl.Buffered(1)),
            pl.BlockSpec((1, gains.shape[1]), lambda i: (0, 0)),
            pl.BlockSpec((tm, HEAD_DIM), lambda i: (i % tab_blocks, 0)),
            pl.BlockSpec((tm, HEAD_DIM), lambda i: (i % tab_blocks, 0)),
        ],
        out_specs=pl.BlockSpec((tm, n), lambda i: (i, 0)),
        scratch_shapes=[pltpu.VMEM((tm, d), BF16)],
        compiler_params=_params("parallel"),
        name="in_proj",
    )(x, gain, w, gains, cos, sin)


def _matmul_residual_kernel(y_ref, w_ref, x_ref, o_ref):
    o_ref[...] = x_ref[...] + jnp.dot(y_ref[...], w_ref[...], preferred_element_type=F32)


def _matmul_residual(y, w, x, *, tm=512):
    t, k = y.shape
    n = w.shape[1]
    tm = min(tm, t)
    assert t % tm == 0
    return pl.pallas_call(
        _matmul_residual_kernel,
        out_shape=jax.ShapeDtypeStruct((t, n), F32),
        grid=(t // tm,),
        in_specs=[
            pl.BlockSpec((tm, k), lambda i: (i, 0)),
            pl.BlockSpec((k, n), lambda i: (0, 0)),
            pl.BlockSpec((tm, n), lambda i: (i, 0)),
        ],
        out_specs=pl.BlockSpec((tm, n), lambda i: (i, 0)),
        compiler_params=_params("parallel"),
        name="matmul_residual",
    )(y, w, x)


def _even_mixer_kernel(u_ref, vn_ref, ga_ref, q_ref, gb_ref,
                       kp_ref, kc_ref, kn_ref, vp_ref, vc_ref, vnx_ref,
                       ws_ref, bs_ref, sink_ref, o_ref, *, nb, blocks):
    t = pl.program_id(1)
    tok = lambda c: slice(c * BLOCK, (c + 1) * BLOCK)

    for g in range(A_GROUPS):
        sl = slice(g * HEAD_DIM, (g + 1) * HEAD_DIM)
        vn_chunks = jnp.concatenate([vn_ref[tok(c), sl] for c in range(blocks)], axis=1)
        mixed = jnp.dot(ws_ref[g], vn_chunks, preferred_element_type=F32)
        for c in range(blocks):
            mixed_c = mixed[:, tok(c)] + bs_ref[g]
            o_ref[tok(c), sl] = (u_ref[tok(c), sl].astype(F32) * mixed_c
                                 * ga_ref[tok(c), sl].astype(F32)).astype(o_ref.dtype)

    k_all = jnp.concatenate([kp_ref[...], kc_ref[...], kn_ref[...]], axis=0)
    v_all = jnp.concatenate([vp_ref[...], vc_ref[...], vnx_ref[...]], axis=0)
    qi = lax.broadcasted_iota(jnp.int32, (BLOCK, 3 * BLOCK), 0)
    kj = lax.broadcasted_iota(jnp.int32, (BLOCK, 3 * BLOCK), 1)
    band = (kj - qi >= 0) & (kj - qi <= 2 * BLOCK)

    for c in range(blocks):
        block_id = t * blocks + c
        first_key = jnp.where(block_id > 0, 0, BLOCK)
        end_key = jnp.where(block_id < nb - 1, 3 * BLOCK, 2 * BLOCK)
        mask = band & (kj >= first_key) & (kj < end_key)
        win = slice(c * BLOCK, (c + 3) * BLOCK)
        for h in range(B_KV_HEADS):
            ksl = slice(h * HEAD_DIM, (h + 1) * HEAD_DIM)
            q_stack = jnp.concatenate(
                [q_ref[tok(c), (h * B_REP + r) * HEAD_DIM:(h * B_REP + r + 1) * HEAD_DIM] for r in range(B_REP)],
                axis=0)
            s = lax.dot_general(q_stack, k_all[win, ksl], (((1,), (1,)), ((), ())),
                                preferred_element_type=F32)
            p_rows, inv_rows = [], []
            for r in range(B_REP):
                sink = sink_ref[h * B_REP + r] * LOG2_E
                s_r = jnp.where(mask, s[r * BLOCK:(r + 1) * BLOCK, :], NEG_BIG)
                m = jnp.maximum(jnp.max(s_r, axis=1, keepdims=True), sink)
                p = jnp.exp2(s_r - m)
                denom = jnp.sum(p, axis=1, keepdims=True) + jnp.exp2(sink - m)
                p_rows.append(p.astype(BF16))
                inv_rows.append(1.0 / denom)
            p_stack = jnp.concatenate(p_rows, axis=0)
            o = jnp.dot(p_stack, v_all[win, ksl], preferred_element_type=F32)
            for r in range(B_REP):
                col = (h * B_REP + r) * HEAD_DIM
                o_r = o[r * BLOCK:(r + 1) * BLOCK, :] * inv_rows[r]
                o_ref[tok(c), A_WIDTH + col:A_WIDTH + col + HEAD_DIM] = (
                    o_r * gb_ref[tok(c), col:col + HEAD_DIM].astype(F32)).astype(o_ref.dtype)


def _even_mixer(z, a_ws, a_bs, sink, *, blocks=4):
    b, s, _ = z.shape
    nb = s // BLOCK
    blocks = min(blocks, nb)
    assert nb % blocks == 0
    rows = blocks * BLOCK
    kv_w = B_KV_HEADS * HEAD_DIM
    wide = lambda c: pl.BlockSpec((None, rows, A_WIDTH), lambda bi, ti: (bi, ti, c))
    prev = lambda c: pl.BlockSpec((None, BLOCK, kv_w), lambda bi, ti: (bi, jnp.maximum(ti * blocks - 1, 0), c))
    nxt = lambda c: pl.BlockSpec((None, BLOCK, kv_w), lambda bi, ti: (bi, jnp.minimum((ti + 1) * blocks, nb - 1), c))
    cur = lambda c: pl.BlockSpec((None, rows, kv_w), lambda bi, ti: (bi, ti, c))
    full = lambda shape: pl.BlockSpec(shape, lambda bi, ti: (0,) * len(shape))
    in_specs = [
        wide(_EVEN_U), wide(_EVEN_V), wide(_EVEN_GA), wide(_EVEN_Q), wide(_EVEN_GB),
        prev(_EVEN_K), cur(_EVEN_K), nxt(_EVEN_K),
        prev(_EVEN_VB), cur(_EVEN_VB), nxt(_EVEN_VB),
        full((A_GROUPS, BLOCK, BLOCK)), full((A_GROUPS, BLOCK, HEAD_DIM)),
        pl.BlockSpec(memory_space=pltpu.SMEM),
    ]
    return pl.pallas_call(
        functools.partial(_even_mixer_kernel, nb=nb, blocks=blocks),
        out_shape=jax.ShapeDtypeStruct((b, s, A_WIDTH + B_WIDTH), BF16),
        grid=(b, nb // blocks),
        in_specs=in_specs,
        out_specs=pl.BlockSpec((None, rows, A_WIDTH + B_WIDTH), lambda bi, ti: (bi, ti, 0)),
        compiler_params=_params("parallel", "parallel"),
        name="even_mixer",
    )(z, z, z, z, z, z, z, z, z, z, z, a_ws, a_bs, sink)


def _flash_kernel(q_ref, q_next_ref, k_ref, v_ref, g_ref, o_ref,
                  qs_ref, qs_next_ref, s_ref, m_ref, acc_ref, *, tq, tk, n_kv):
    qi = pl.program_id(2)
    n_lane_tiles = tk // HEAD_DIM
    ones = jnp.ones((tk, HEAD_DIM), BF16)

    def stack_heads(src_ref, dst_ref):
        for r in range(C_REP):
            dst_ref[r * tq:(r + 1) * tq, :] = src_ref[:, r * HEAD_DIM:(r + 1) * HEAD_DIM]

    def scores(stacked_ref, j):
        start = pl.multiple_of(j * tk, tk)
        return lax.dot_general(stacked_ref[...], k_ref[pl.ds(start, tk), :], (((1,), (1,)), ((), ())),
                               preferred_element_type=F32)

    @pl.when(qi == 0)
    def _():
        stack_heads(q_ref, qs_ref)
        s_ref[0] = scores(qs_ref, 0)

    @pl.when(qi != 0)
    def _():
        qs_ref[...] = qs_next_ref[...]

    m_ref[...] = jnp.full(m_ref.shape, NEG_BIG, F32)
    acc_ref[...] = jnp.zeros(acc_ref.shape, F32)

    def softmax_pv(slot, j):
        start = pl.multiple_of(j * tk, tk)
        v_ext = jnp.concatenate([v_ref[pl.ds(start, tk), :], ones], axis=1)
        tiles = [s_ref[slot, :, c * HEAD_DIM:(c + 1) * HEAD_DIM] for c in range(n_lane_tiles)]
        m_prev = m_ref[...]
        m_new = jnp.maximum(m_prev, jnp.max(functools.reduce(jnp.maximum, tiles), axis=1, keepdims=True))
        alpha = jnp.exp2(m_prev - m_new)
        p = jnp.concatenate([jnp.exp2(tile - m_new).astype(BF16) for tile in tiles], axis=1)
        pv = jnp.dot(p, v_ext, preferred_element_type=F32)
        acc_ref[...] = jnp.concatenate([alpha, alpha], axis=1) * acc_ref[...] + pv
        m_ref[...] = m_new

    def run_pair(j0, last):
        s_ref[1] = scores(qs_ref, j0 + 1)
        softmax_pv(0, j0)
        if last:
            stack_heads(q_next_ref, qs_next_ref)
            s_ref[0] = scores(qs_next_ref, 0)
        else:
            s_ref[0] = scores(qs_ref, j0 + 2)
        softmax_pv(1, j0 + 1)

    def body(jj, carry):
        run_pair(2 * jj, False)
        return carry

    lax.fori_loop(0, n_kv // 2 - 1, body, 0)
    run_pair(n_kv - 2, True)

    for r in range(C_REP):
        sl = slice(r * HEAD_DIM, (r + 1) * HEAD_DIM)
        rows = slice(r * tq, (r + 1) * tq)
        o_r = acc_ref[rows, :HEAD_DIM] / acc_ref[rows, HEAD_DIM:]
        o_ref[:, sl] = (o_r * g_ref[:, sl].astype(F32)).astype(o_ref.dtype)


def _flash(z, *, tq=512, tk=512):
    b, s, _ = z.shape
    tq, tk = min(tq, s), min(tk, s)
    n_kv = s // tk
    assert s % tq == 0 and s % tk == 0 and n_kv % 2 == 0
    qw = C_REP * HEAD_DIM
    rows = C_REP * tq
    nq = s // tq
    return pl.pallas_call(
        functools.partial(_flash_kernel, tq=tq, tk=tk, n_kv=n_kv),
        out_shape=jax.ShapeDtypeStruct((b, s, C_WIDTH), BF16),
        grid=(b, C_KV_HEADS, nq),
        in_specs=[
            pl.BlockSpec((None, tq, qw), lambda bi, hi, qi: (bi, qi, hi)),
            pl.BlockSpec((None, tq, qw), lambda bi, hi, qi: (bi, jnp.minimum(qi + 1, nq - 1), hi)),
            pl.BlockSpec((None, s, HEAD_DIM), lambda bi, hi, qi: (bi, 0, _ODD_K_BLK128 + hi)),
            pl.BlockSpec((None, s, HEAD_DIM), lambda bi, hi, qi: (bi, 0, _ODD_V_BLK128 + hi)),
            pl.BlockSpec((None, tq, qw), lambda bi, hi, qi: (bi, qi, _ODD_G_BLK512 + hi)),
        ],
        out_specs=pl.BlockSpec((None, tq, qw), lambda bi, hi, qi: (bi, qi, hi)),
        scratch_shapes=[
            pltpu.VMEM((rows, HEAD_DIM), BF16),
            pltpu.VMEM((rows, HEAD_DIM), BF16),
            pltpu.VMEM((2, rows, tk), F32),
            pltpu.VMEM((rows, HEAD_DIM), F32),
            pltpu.VMEM((rows, 2 * HEAD_DIM), F32),
        ],
        compiler_params=_params("parallel", "parallel", "arbitrary"),
        name="flash_gqa",
    )(z, z, z, z, z)


def _reorder_heads(a, segments):
    lead = a.shape[:-1]
    x = a.reshape(lead + (a.shape[-1] // HEAD_DIM, HEAD_DIM))
    x = jnp.concatenate([x[..., lo:hi] for lo, hi in segments], axis=-1)
    return x.reshape(a.shape)


def _rope_tables(lane_pos, theta, half, rotated):
    lane = np.arange(HEAD_DIM)
    inv = jnp.power(F32(theta), -jnp.asarray(lane % half, F32) * (1.0 / half))
    ang = lane_pos * inv[None, :]
    sign = jnp.asarray(np.where(lane < HEAD_DIM // 2, -1.0, 1.0), F32)[None, :]
    rot = jnp.asarray(rotated)[None, :]
    return jnp.where(rot, jnp.cos(ang), 1.0), jnp.where(rot, jnp.sin(ang) * sign, 0.0)


def _trunk(x, tables, even_w, odd_w):
    b, s, d = x.shape
    (cos_b, sin_b), (cos_c, sin_c) = tables
    cos_b, sin_b, cos_c, sin_c = cos_b[:s], sin_b[:s], cos_c[:s], sin_c[:s]
    xf = x.reshape(b * s, d)
    for layer in range(len(even_w) + len(odd_w)):
        i = layer // 2
        if layer % 2 == 0:
            norm_g, w_in, gains, w_out, a_ws, a_bs, sink = even_w[i]
            z = _in_proj(xf, norm_g, w_in, gains, cos_b, sin_b, _EVEN_CHUNKS, seq=s).reshape(b, s, AB_IN)
            y = _even_mixer(z, a_ws, a_bs, sink)
        else:
            norm_g, w_in, gains, w_out = odd_w[i]
            z = _in_proj(xf, norm_g, w_in, gains, cos_c, sin_c, _ODD_CHUNKS, seq=s).reshape(b, s, C_IN)
            y = _flash(z)
        xf = _matmul_residual(y.reshape(b * s, d), w_out, xf)
    return xf.reshape(b, s, d)


def kernel(x_prompt, x_sample, norm_ab, w_in_ab, w_out_ab, a_v_norm, a_w_s, a_b_s, b_q_norm, b_k_norm,
           b_sink, norm_c, w_in_c, w_out_c, c_q_norm, c_k_norm):
    s_max = max(x_prompt.shape[1], x_sample.shape[1])
    pos = jnp.arange(s_max, dtype=F32)
    row = jnp.floor(pos / GRID_W)
    col = pos - row * GRID_W
    lane = np.arange(HEAD_DIM)
    row_lane = (lane // (HEAD_DIM // 4)) % 2 == 0
    tables = (
        _rope_tables(pos[:, None], ROPE_THETA, ROPE_DIMS // 2, lane % (HEAD_DIM // 2) < ROPE_DIMS // 2),
        _rope_tables(jnp.where(jnp.asarray(row_lane)[None, :], row[:, None], col[:, None]),
                     AXIAL_THETA, HEAD_DIM // 4, np.ones(HEAD_DIM, bool)),
    )

    kv_b = B_KV_HEADS * HEAD_DIM
    kv_c = C_KV_HEADS * HEAD_DIM
    c0 = 3 * A_WIDTH
    c1 = c0 + B_WIDTH
    c2 = c1 + kv_b
    c3 = c2 + kv_b
    even_w = []
    for i in range(norm_ab.shape[0]):
        w = w_in_ab[i]
        w_cols = jnp.concatenate([
            w[:, :c0], _reorder_heads(w[:, c0:c1], _PARTIAL_SEGMENTS), w[:, c3:],
            _reorder_heads(w[:, c1:c2], _PARTIAL_SEGMENTS), w[:, c2:c3]], axis=1).astype(BF16)
        q_gain = _reorder_heads(b_q_norm[i], _PARTIAL_SEGMENTS) * Q_SCALE
        k_gain = _reorder_heads(b_k_norm[i], _PARTIAL_SEGMENTS)
        gains = jnp.concatenate([a_v_norm[i], q_gain, k_gain])[None, :]
        even_w.append((
            norm_ab[i][None, :], w_cols, gains, w_out_ab[i].astype(BF16), a_w_s[i].astype(BF16),
            jnp.broadcast_to(a_b_s[i][:, :, None], (A_GROUPS, BLOCK, HEAD_DIM)), b_sink[i],
        ))
    odd_w = []
    for i in range(norm_c.shape[0]):
        w = w_in_c[i]
        qk_end = C_WIDTH + kv_c
        w_cols = jnp.concatenate([_reorder_heads(w[:, :qk_end], _AXIAL_SEGMENTS), w[:, qk_end:]],
                                 axis=1).astype(BF16)
        q_gain = _reorder_heads(c_q_norm[i], _AXIAL_SEGMENTS) * Q_SCALE
        k_gain = _reorder_heads(c_k_norm[i], _AXIAL_SEGMENTS)
        gains = jnp.concatenate([q_gain, k_gain])[None, :]
        odd_w.append((norm_c[i][None, :], w_cols, gains, w_out_c[i].astype(BF16)))

    y_prompt = _trunk(x_prompt, tables, even_w, odd_w)
    y_sample = _trunk(x_sample, tables, even_w, odd_w)
    return (y_prompt, y_sample)
```

```python
import functools

import jax
import jax.numpy as jnp
import numpy as np
from jax import lax
from jax.experimental import pallas as pl
from jax.experimental.pallas import tpu as pltpu

D_MODEL = 2048
HEAD_DIM = 128
BLOCK = 128
GRID_W = 64
EPS = 1e-6
A_WIDTH = D_MODEL // 2
A_GROUPS = A_WIDTH // HEAD_DIM
B_WIDTH = D_MODEL // 2
B_HEADS = B_WIDTH // HEAD_DIM
B_KV_HEADS = 2
B_REP = B_HEADS // B_KV_HEADS
ROPE_THETA = 500000.0
ROPE_DIMS = HEAD_DIM // 4
C_WIDTH = D_MODEL
C_HEADS = C_WIDTH // HEAD_DIM
C_KV_HEADS = 4
C_REP = C_HEADS // C_KV_HEADS
AXIAL_THETA = 10000.0
AB_IN = 3 * A_WIDTH + 2 * B_WIDTH + 2 * B_KV_HEADS * HEAD_DIM
C_IN = 2 * C_WIDTH + 2 * C_KV_HEADS * HEAD_DIM
NEG_BIG = -1e30
LOG2_E = 1.4426950408889634
Q_SCALE = HEAD_DIM ** -0.5 * LOG2_E

VMEM_LIMIT_BYTES = 56 * 1024 * 1024
PROJ_TN = 256

F32 = jnp.float32
BF16 = jnp.bfloat16

_AXIAL_SEGMENTS = ((0, 32), (64, 96), (32, 64), (96, 128))
_PARTIAL_SEGMENTS = ((0, 16), (32, 80), (16, 32), (80, 128))

_EVEN_U, _EVEN_V, _EVEN_GA, _EVEN_Q, _EVEN_GB = 0, 1, 2, 3, 4
_EVEN_K = (4 * A_WIDTH + B_WIDTH) // (B_KV_HEADS * HEAD_DIM)
_EVEN_VB = _EVEN_K + 1
_EVEN_Q_GAIN = A_WIDTH
_EVEN_K_GAIN = A_WIDTH + HEAD_DIM
_EVEN_CHUNKS = ((("gelu", None),) * A_GROUPS
                + tuple(("gelu_norm", g * HEAD_DIM) for g in range(A_GROUPS))
                + (("silu", None),) * A_GROUPS
                + (("norm_rope", _EVEN_Q_GAIN),) * B_HEADS
                + (("silu", None),) * B_HEADS
                + (("norm_rope", _EVEN_K_GAIN),) * B_KV_HEADS
                + (("none", None),) * B_KV_HEADS)
_ODD_K_BLK128 = C_WIDTH // HEAD_DIM
_ODD_V_BLK128 = (C_WIDTH + C_KV_HEADS * HEAD_DIM) // HEAD_DIM
_ODD_G_BLK512 = (C_WIDTH + 2 * C_KV_HEADS * HEAD_DIM) // 512
_ODD_CHUNKS = ((("norm_rope", 0),) * C_HEADS + (("norm_rope", HEAD_DIM),) * C_KV_HEADS
               + (("none", None),) * C_KV_HEADS + (("silu", None),) * C_HEADS)


def _params(*sem):
    return pltpu.CompilerParams(dimension_semantics=sem, vmem_limit_bytes=VMEM_LIMIT_BYTES)


def _gelu(x):
    return 0.5 * x * (1.0 + jnp.tanh(0.7978845608028654 * (x + 0.044715 * (x * x * x))))


def _silu(x):
    return x / (1.0 + jnp.exp(-x))


def _head_norm(x, gain):
    ms = jnp.mean(x * x, axis=-1, keepdims=True)
    return x * lax.rsqrt(ms + EPS) * gain


def _epilogue(kind, x, gain, cos_ref, sin_ref):
    if kind == "none":
        return x
    if kind == "gelu":
        return _gelu(x)
    if kind == "silu":
        return _silu(x)
    if kind == "gelu_norm":
        return _head_norm(_gelu(x), gain)
    assert kind == "norm_rope"
    xn = _head_norm(x, gain)
    return xn * cos_ref[...] + pltpu.roll(xn, HEAD_DIM // 2, axis=1) * sin_ref[...]


def _in_proj_kernel(x_ref, g_ref, w_ref, gains_ref, cos_ref, sin_ref, o_ref, h_ref, *, chunks):
    x = x_ref[...]
    ms = jnp.mean(x * x, axis=-1, keepdims=True)
    h_ref[...] = (x * lax.rsqrt(ms + EPS) * g_ref[...]).astype(BF16)
    chunks_per_dot = PROJ_TN // HEAD_DIM
    cost = {"norm_rope": 0, "gelu_norm": 1, "none": 2, "gelu": 3, "silu": 4}
    starts = sorted(range(0, len(chunks), chunks_per_dot), key=lambda f: (cost[chunks[f][0]], f))
    for first in starts:
        base = first * HEAD_DIM
        acc = jnp.dot(h_ref[...], w_ref[:, base:base + PROJ_TN], preferred_element_type=F32)
        for c, (kind, gain_at) in enumerate(chunks[first:first + chunks_per_dot]):
            sl = slice(base + c * HEAD_DIM, base + (c + 1) * HEAD_DIM)
            gain = None if gain_at is None else gains_ref[:, gain_at:gain_at + HEAD_DIM]
            o_ref[:, sl] = _epilogue(kind, acc[:, c * HEAD_DIM:(c + 1) * HEAD_DIM], gain,
                                     cos_ref, sin_ref).astype(o_ref.dtype)


def _in_proj(x, gain, w, gains, cos, sin, chunks, *, seq, tm=512):
    t, d = x.shape
    n = w.shape[1]
    tm = min(tm, seq)
    assert t % tm == 0 and seq % tm == 0 and n == HEAD_DIM * len(chunks) and n % PROJ_TN == 0
    tab_blocks = seq // tm
    return pl.pallas_call(
        functools.partial(_in_proj_kernel, chunks=chunks),
        out_shape=jax.ShapeDtypeStruct((t, n), BF16),
        grid=(t // tm,),
        in_specs=[
            pl.BlockSpec((tm, d), lambda i: (i, 0)),
            pl.BlockSpec((1, d), lambda i: (0, 0)),
            pl.BlockSpec((d, n), lambda i: (0, 0), pipeline_mode=pl.Buffered(1)),
            pl.BlockSpec((1, gains.shape[1]), lambda i: (0, 0)),
            pl.BlockSpec((tm, HEAD_DIM), lambda i: (i % tab_blocks, 0)),
            pl.BlockSpec((tm, HEAD_DIM), lambda i: (i % tab_blocks, 0)),
        ],
        out_specs=pl.BlockSpec((tm, n), lambda i: (i, 0)),
        scratch_shapes=[pltpu.VMEM((tm, d), BF16)],
        compiler_params=_params("parallel"),
        name="in_proj",
    )(x, gain, w, gains, cos, sin)


def _matmul_residual_kernel(y_ref, w_ref, x_ref, o_ref):
    o_ref[...] = x_ref[...] + jnp.dot(y_ref[...], w_ref[...], preferred_element_type=F32)


def _matmul_residual(y, w, x, *, tm=512):
    t, k = y.shape
    n = w.shape[1]
    tm = min(tm, t)
    assert t % tm == 0
    return pl.pallas_call(
        _matmul_residual_kernel,
        out_shape=jax.ShapeDtypeStruct((t, n), F32),
        grid=(t // tm,),
        in_specs=[
            pl.BlockSpec((tm, k), lambda i: (i, 0)),
            pl.BlockSpec((k, n), lambda i: (0, 0)),
            pl.BlockSpec((tm, n), lambda i: (i, 0)),
        ],
        out_specs=pl.BlockSpec((tm, n), lambda i: (i, 0)),
        compiler_params=_params("parallel"),
        name="matmul_residual",
    )(y, w, x)


def _even_mixer_kernel(u_ref, vn_ref, ga_ref, q_ref, gb_ref,
                       kp_ref, kc_ref, kn_ref, vp_ref, vc_ref, vnx_ref,
                       ws_ref, bs_ref, sink_ref, o_ref, *, nb, blocks):
    t = pl.program_id(1)
    tok = lambda c: slice(c * BLOCK, (c + 1) * BLOCK)

    for g in range(A_GROUPS):
        sl = slice(g * HEAD_DIM, (g + 1) * HEAD_DIM)
        vn_chunks = jnp.concatenate([vn_ref[tok(c), sl] for c in range(blocks)], axis=1)
        mixed = jnp.dot(ws_ref[g], vn_chunks, preferred_element_type=F32)
        for c in range(blocks):
            mixed_c = mixed[:, tok(c)] + bs_ref[g]
            o_ref[tok(c), sl] = (u_ref[tok(c), sl].astype(F32) * mixed_c
                                 * ga_ref[tok(c), sl].astype(F32)).astype(o_ref.dtype)

    k_all = jnp.concatenate([kp_ref[...], kc_ref[...], kn_ref[...]], axis=0)
    v_all = jnp.concatenate([vp_ref[...], vc_ref[...], vnx_ref[...]], axis=0)
    qi = lax.broadcasted_iota(jnp.int32, (BLOCK, 3 * BLOCK), 0)
    kj = lax.broadcasted_iota(jnp.int32, (BLOCK, 3 * BLOCK), 1)
    band = (kj - qi >= 0) & (kj - qi <= 2 * BLOCK)

    for c in range(blocks):
        block_id = t * blocks + c
        first_key = jnp.where(block_id > 0, 0, BLOCK)
        end_key = jnp.where(block_id < nb - 1, 3 * BLOCK, 2 * BLOCK)
        mask = band & (kj >= first_key) & (kj < end_key)
        win = slice(c * BLOCK, (c + 3) * BLOCK)
        for h in range(B_KV_HEADS):
            ksl = slice(h * HEAD_DIM, (h + 1) * HEAD_DIM)
            q_stack = jnp.concatenate(
                [q_ref[tok(c), (h * B_REP + r) * HEAD_DIM:(h * B_REP + r + 1) * HEAD_DIM] for r in range(B_REP)],
                axis=0)
            s = lax.dot_general(q_stack, k_all[win, ksl], (((1,), (1,)), ((), ())),
                                preferred_element_type=F32)
            p_rows, inv_rows = [], []
            for r in range(B_REP):
                sink = sink_ref[h * B_REP + r] * LOG2_E
                s_r = jnp.where(mask, s[r * BLOCK:(r + 1) * BLOCK, :], NEG_BIG)
                m = jnp.maximum(jnp.max(s_r, axis=1, keepdims=True), sink)
                p = jnp.exp2(s_r - m)
                denom = jnp.sum(p, axis=1, keepdims=True) + jnp.exp2(sink - m)
                p_rows.append(p.astype(BF16))
                inv_rows.append(1.0 / denom)
            p_stack = jnp.concatenate(p_rows, axis=0)
            o = jnp.dot(p_stack, v_all[win, ksl], preferred_element_type=F32)
            for r in range(B_REP):
                col = (h * B_REP + r) * HEAD_DIM
                o_r = o[r * BLOCK:(r + 1) * BLOCK, :] * inv_rows[r]
                o_ref[tok(c), A_WIDTH + col:A_WIDTH + col + HEAD_DIM] = (
                    o_r * gb_ref[tok(c), col:col + HEAD_DIM].astype(F32)).astype(o_ref.dtype)


def _even_mixer(z, a_ws, a_bs, sink, *, blocks=4):
    b, s, _ = z.shape
    nb = s // BLOCK
    blocks = min(blocks, nb)
    assert nb % blocks == 0
    rows = blocks * BLOCK
    kv_w = B_KV_HEADS * HEAD_DIM
    wide = lambda c: pl.BlockSpec((None, rows, A_WIDTH), lambda bi, ti: (bi, ti, c))
    prev = lambda c: pl.BlockSpec((None, BLOCK, kv_w), lambda bi, ti: (bi, jnp.maximum(ti * blocks - 1, 0), c))
    nxt = lambda c: pl.BlockSpec((None, BLOCK, kv_w), lambda bi, ti: (bi, jnp.minimum((ti + 1) * blocks, nb - 1), c))
    cur = lambda c: pl.BlockSpec((None, rows, kv_w), lambda bi, ti: (bi, ti, c))
    full = lambda shape: pl.BlockSpec(shape, lambda bi, ti: (0,) * len(shape))
    in_specs = [
        wide(_EVEN_U), wide(_EVEN_V), wide(_EVEN_GA), wide(_EVEN_Q), wide(_EVEN_GB),
        prev(_EVEN_K), cur(_EVEN_K), nxt(_EVEN_K),
        prev(_EVEN_VB), cur(_EVEN_VB), nxt(_EVEN_VB),
        full((A_GROUPS, BLOCK, BLOCK)), full((A_GROUPS, BLOCK, HEAD_DIM)),
        pl.BlockSpec(memory_space=pltpu.SMEM),
    ]
    return pl.pallas_call(
        functools.partial(_even_mixer_kernel, nb=nb, blocks=blocks),
        out_shape=jax.ShapeDtypeStruct((b, s, A_WIDTH + B_WIDTH), BF16),
        grid=(b, nb // blocks),
        in_specs=in_specs,
        out_specs=pl.BlockSpec((None, rows, A_WIDTH + B_WIDTH), lambda bi, ti: (bi, ti, 0)),
        compiler_params=_params("parallel", "parallel"),
        name="even_mixer",
    )(z, z, z, z, z, z, z, z, z, z, z, a_ws, a_bs, sink)


def _flash_kernel(q_ref, q_next_ref, k_ref, v_ref, g_ref, o_ref,
                  qs_ref, qs_next_ref, s_ref, m_ref, acc_ref, *, tq, tk, n_kv):
    qi = pl.program_id(2)
    n_lane_tiles = tk // HEAD_DIM
    ones = jnp.ones((tk, HEAD_DIM), BF16)

    def stack_heads(src_ref, dst_ref):
        for r in range(C_REP):
            dst_ref[r * tq:(r + 1) * tq, :] = src_ref[:, r * HEAD_DIM:(r + 1) * HEAD_DIM]

    def scores(stacked_ref, j):
        start = pl.multiple_of(j * tk, tk)
        return lax.dot_general(stacked_ref[...], k_ref[pl.ds(start, tk), :], (((1,), (1,)), ((), ())),
                               preferred_element_type=F32)

    @pl.when(qi == 0)
    def _():
        stack_heads(q_ref, qs_ref)
        s_ref[0] = scores(qs_ref, 0)

    @pl.when(qi != 0)
    def _():
        qs_ref[...] = qs_next_ref[...]

    def softmax_pv(slot, j, first):
        start = pl.multiple_of(j * tk, tk)
        v_ext = jnp.concatenate([v_ref[pl.ds(start, tk), :], ones], axis=1)
        tiles = [s_ref[slot, :, c * HEAD_DIM:(c + 1) * HEAD_DIM] for c in range(n_lane_tiles)]
        m_cur = jnp.max(functools.reduce(jnp.maximum, tiles), axis=1, keepdims=True)
        m_new = jnp.broadcast_to(m_cur, m_ref.shape) if first else jnp.maximum(m_ref[...], m_cur)
        p = jnp.concatenate([jnp.exp2(tile - m_new).astype(BF16) for tile in tiles], axis=1)
        pv = jnp.dot(p, v_ext, preferred_element_type=F32)
        if first:
            acc_ref[...] = pv
        else:
            alpha = jnp.exp2(m_ref[...] - m_new)
            acc_ref[...] = jnp.concatenate([alpha, alpha], axis=1) * acc_ref[...] + pv
        m_ref[...] = m_new

    def run_pair(j0, first, last):
        s_ref[1] = scores(qs_ref, j0 + 1)
        softmax_pv(0, j0, first)
        if last:
            stack_heads(q_next_ref, qs_next_ref)
            s_ref[0] = scores(qs_next_ref, 0)
        else:
            s_ref[0] = scores(qs_ref, j0 + 2)
        softmax_pv(1, j0 + 1, False)

    def body(jj, carry):
        run_pair(2 * jj, False, False)
        return carry

    n_pairs = n_kv // 2
    run_pair(0, True, n_pairs == 1)
    lax.fori_loop(1, n_pairs - 1, body, 0)
    if n_pairs > 1:
        run_pair(n_kv - 2, False, True)

    for r in range(C_REP):
        sl = slice(r * HEAD_DIM, (r + 1) * HEAD_DIM)
        rows = slice(r * tq, (r + 1) * tq)
        o_r = acc_ref[rows, :HEAD_DIM] / acc_ref[rows, HEAD_DIM:]
        o_ref[:, sl] = (o_r * g_ref[:, sl].astype(F32)).astype(o_ref.dtype)


def _flash(z, *, tq=512, tk=512):
    b, s, _ = z.shape
    tq, tk = min(tq, s), min(tk, s)
    n_kv = s // tk
    assert s % tq == 0 and s % tk == 0 and n_kv % 2 == 0
    qw = C_REP * HEAD_DIM
    rows = C_REP * tq
    nq = s // tq
    return pl.pallas_call(
        functools.partial(_flash_kernel, tq=tq, tk=tk, n_kv=n_kv),
        out_shape=jax.ShapeDtypeStruct((b, s, C_WIDTH), BF16),
        grid=(b, C_KV_HEADS, nq),
        in_specs=[
            pl.BlockSpec((None, tq, qw), lambda bi, hi, qi: (bi, qi, hi)),
            pl.BlockSpec((None, tq, qw), lambda bi, hi, qi: (bi, jnp.minimum(qi + 1, nq - 1), hi)),
            pl.BlockSpec((None, s, HEAD_DIM), lambda bi, hi, qi: (bi, 0, _ODD_K_BLK128 + hi)),
            pl.BlockSpec((None, s, HEAD_DIM), lambda bi, hi, qi: (bi, 0, _ODD_V_BLK128 + hi)),
            pl.BlockSpec((None, tq, qw), lambda bi, hi, qi: (bi, qi, _ODD_G_BLK512 + hi)),
        ],
        out_specs=pl.BlockSpec((None, tq, qw), lambda bi, hi, qi: (bi, qi, hi)),
        scratch_shapes=[
            pltpu.VMEM((rows, HEAD_DIM), BF16),
            pltpu.VMEM((rows, HEAD_DIM), BF16),
            pltpu.VMEM((2, rows, tk), F32),
            pltpu.VMEM((rows, HEAD_DIM), F32),
            pltpu.VMEM((rows, 2 * HEAD_DIM), F32),
        ],
        compiler_params=_params("parallel", "parallel", "arbitrary"),
        name="flash_gqa",
    )(z, z, z, z, z)


def _reorder_heads(a, segments):
    lead = a.shape[:-1]
    x = a.reshape(lead + (a.shape[-1] // HEAD_DIM, HEAD_DIM))
    x = jnp.concatenate([x[..., lo:hi] for lo, hi in segments], axis=-1)
    return x.reshape(a.shape)


def _rope_tables(lane_pos, theta, half, rotated):
    lane = np.arange(HEAD_DIM)
    inv = jnp.power(F32(theta), -jnp.asarray(lane % half, F32) * (1.0 / half))
    ang = lane_pos * inv[None, :]
    sign = jnp.asarray(np.where(lane < HEAD_DIM // 2, -1.0, 1.0), F32)[None, :]
    rot = jnp.asarray(rotated)[None, :]
    return jnp.where(rot, jnp.cos(ang), 1.0), jnp.where(rot, jnp.sin(ang) * sign, 0.0)


def _trunk(x, tables, even_w, odd_w):
    b, s, d = x.shape
    (cos_b, sin_b), (cos_c, sin_c) = tables
    cos_b, sin_b, cos_c, sin_c = cos_b[:s], sin_b[:s], cos_c[:s], sin_c[:s]
    xf = x.reshape(b * s, d)
    for layer in range(len(even_w) + len(odd_w)):
        i = layer // 2
        if layer % 2 == 0:
            norm_g, w_in, gains, w_out, a_ws, a_bs, sink = even_w[i]
            z = _in_proj(xf, norm_g, w_in, gains, cos_b, sin_b, _EVEN_CHUNKS, seq=s).reshape(b, s, AB_IN)
            y = _even_mixer(z, a_ws, a_bs, sink)
        else:
            norm_g, w_in, gains, w_out = odd_w[i]
            z = _in_proj(xf, norm_g, w_in, gains, cos_c, sin_c, _ODD_CHUNKS, seq=s).reshape(b, s, C_IN)
            y = _flash(z)
        xf = _matmul_residual(y.reshape(b * s, d), w_out, xf)
    return xf.reshape(b, s, d)


def kernel(x_prompt, x_sample, norm_ab, w_in_ab, w_out_ab, a_v_norm, a_w_s, a_b_s, b_q_norm, b_k_norm,
           b_sink, norm_c, w_in_c, w_out_c, c_q_norm, c_k_norm):
    s_max = max(x_prompt.shape[1], x_sample.shape[1])
    pos = jnp.arange(s_max, dtype=F32)
    row = jnp.floor(pos / GRID_W)
    col = pos - row * GRID_W
    lane = np.arange(HEAD_DIM)
    row_lane = (lane // (HEAD_DIM // 4)) % 2 == 0
    tables = (
        _rope_tables(pos[:, None], ROPE_THETA, ROPE_DIMS // 2, lane % (HEAD_DIM // 2) < ROPE_DIMS // 2),
        _rope_tables(jnp.where(jnp.asarray(row_lane)[None, :], row[:, None], col[:, None]),
                     AXIAL_THETA, HEAD_DIM // 4, np.ones(HEAD_DIM, bool)),
    )

    kv_b = B_KV_HEADS * HEAD_DIM
    kv_c = C_KV_HEADS * HEAD_DIM
    c0 = 3 * A_WIDTH
    c1 = c0 + B_WIDTH
    c2 = c1 + kv_b
    c3 = c2 + kv_b
    even_w = []
    for i in range(norm_ab.shape[0]):
        w = w_in_ab[i]
        w_cols = jnp.concatenate([
            w[:, :c0], _reorder_heads(w[:, c0:c1], _PARTIAL_SEGMENTS), w[:, c3:],
            _reorder_heads(w[:, c1:c2], _PARTIAL_SEGMENTS), w[:, c2:c3]], axis=1).astype(BF16)
        q_gain = _reorder_heads(b_q_norm[i], _PARTIAL_SEGMENTS) * Q_SCALE
        k_gain = _reorder_heads(b_k_norm[i], _PARTIAL_SEGMENTS)
        gains = jnp.concatenate([a_v_norm[i], q_gain, k_gain])[None, :]
        even_w.append((
            norm_ab[i][None, :], w_cols, gains, w_out_ab[i].astype(BF16), a_w_s[i].astype(BF16),
            jnp.broadcast_to(a_b_s[i][:, :, None], (A_GROUPS, BLOCK, HEAD_DIM)), b_sink[i],
        ))
    odd_w = []
    for i in range(norm_c.shape[0]):
        w = w_in_c[i]
        qk_end = C_WIDTH + kv_c
        w_cols = jnp.concatenate([_reorder_heads(w[:, :qk_end], _AXIAL_SEGMENTS), w[:, qk_end:]],
                                 axis=1).astype(BF16)
        q_gain = _reorder_heads(c_q_norm[i], _AXIAL_SEGMENTS) * Q_SCALE
        k_gain = _reorder_heads(c_k_norm[i], _AXIAL_SEGMENTS)
        gains = jnp.concatenate([q_gain, k_gain])[None, :]
        odd_w.append((norm_c[i][None, :], w_cols, gains, w_out_c[i].astype(BF16)))

    y_prompt = _trunk(x_prompt, tables, even_w, odd_w)
    y_sample = _trunk(x_sample, tables, even_w, odd_w)
    return (y_prompt, y_sample)
```

```python
import functools

import jax
import jax.numpy as jnp
import numpy as np
from jax import lax
from jax.experimental import pallas as pl
from jax.experimental.pallas import tpu as pltpu

D_MODEL = 2048
HEAD_DIM = 128
BLOCK = 128
GRID_W = 64
EPS = 1e-6
A_WIDTH = D_MODEL // 2
A_GROUPS = A_WIDTH // HEAD_DIM
B_WIDTH = D_MODEL // 2
B_HEADS = B_WIDTH // HEAD_DIM
B_KV_HEADS = 2
B_REP = B_HEADS // B_KV_HEADS
ROPE_THETA = 500000.0
ROPE_DIMS = HEAD_DIM // 4
C_WIDTH = D_MODEL
C_HEADS = C_WIDTH // HEAD_DIM
C_KV_HEADS = 4
C_REP = C_HEADS // C_KV_HEADS
AXIAL_THETA = 10000.0
AB_IN = 3 * A_WIDTH + 2 * B_WIDTH + 2 * B_KV_HEADS * HEAD_DIM
C_IN = 2 * C_WIDTH + 2 * C_KV_HEADS * HEAD_DIM
NEG_BIG = -1e30
LOG2_E = 1.4426950408889634
Q_SCALE = HEAD_DIM ** -0.5 * LOG2_E

VMEM_LIMIT_BYTES = 56 * 1024 * 1024
PROJ_TN = 256

F32 = jnp.float32
BF16 = jnp.bfloat16

_AXIAL_SEGMENTS = ((0, 32), (64, 96), (32, 64), (96, 128))
_PARTIAL_SEGMENTS = ((0, 16), (32, 80), (16, 32), (80, 128))

_EVEN_U, _EVEN_V, _EVEN_GA, _EVEN_Q, _EVEN_GB = 0, 1, 2, 3, 4
_EVEN_K = (4 * A_WIDTH + B_WIDTH) // (B_KV_HEADS * HEAD_DIM)
_EVEN_VB = _EVEN_K + 1
_EVEN_Q_GAIN = A_WIDTH
_EVEN_K_GAIN = A_WIDTH + HEAD_DIM
_EVEN_CHUNKS = ((("gelu", None),) * A_GROUPS
                + tuple(("gelu_norm", g * HEAD_DIM) for g in range(A_GROUPS))
                + (("silu", None),) * A_GROUPS
                + (("norm_rope", _EVEN_Q_GAIN),) * B_HEADS
                + (("silu", None),) * B_HEADS
                + (("norm_rope", _EVEN_K_GAIN),) * B_KV_HEADS
                + (("none", None),) * B_KV_HEADS)
_ODD_K_BLK128 = C_WIDTH // HEAD_DIM
_ODD_V_BLK128 = (C_WIDTH + C_KV_HEADS * HEAD_DIM) // HEAD_DIM
_ODD_G_BLK512 = (C_WIDTH + 2 * C_KV_HEADS * HEAD_DIM) // 512
_ODD_CHUNKS = ((("norm_rope", 0),) * C_HEADS + (("norm_rope", HEAD_DIM),) * C_KV_HEADS
               + (("none", None),) * C_KV_HEADS + (("silu", None),) * C_HEADS)


def _params(*sem):
    return pltpu.CompilerParams(dimension_semantics=sem, vmem_limit_bytes=VMEM_LIMIT_BYTES)


def _gelu(x):
    return 0.5 * x * (1.0 + jnp.tanh(0.7978845608028654 * (x + 0.044715 * (x * x * x))))


def _silu(x):
    return x / (1.0 + jnp.exp(-x))


def _head_norm(x, gain):
    ms = jnp.mean(x * x, axis=-1, keepdims=True)
    return x * lax.rsqrt(ms + EPS) * gain


def _epilogue(kind, x, gain, cos_ref, sin_ref):
    if kind == "none":
        return x
    if kind == "gelu":
        return _gelu(x)
    if kind == "silu":
        return _silu(x)
    if kind == "gelu_norm":
        return _head_norm(_gelu(x), gain)
    assert kind == "norm_rope"
    xn = _head_norm(x, gain)
    return xn * cos_ref[...] + pltpu.roll(xn, HEAD_DIM // 2, axis=1) * sin_ref[...]


def _in_proj_kernel(x_ref, g_ref, w_ref, gains_ref, cos_ref, sin_ref, o_ref, h_ref, *, chunks):
    x = x_ref[...]
    ms = jnp.mean(x * x, axis=-1, keepdims=True)
    h_ref[...] = (x * lax.rsqrt(ms + EPS) * g_ref[...]).astype(BF16)
    chunks_per_dot = PROJ_TN // HEAD_DIM
    cost = {"norm_rope": 0, "gelu_norm": 1, "none": 2, "gelu": 3, "silu": 4}
    starts = sorted(range(0, len(chunks), chunks_per_dot), key=lambda f: (cost[chunks[f][0]], f))
    for first in starts:
        base = first * HEAD_DIM
        acc = jnp.dot(h_ref[...], w_ref[:, base:base + PROJ_TN], preferred_element_type=F32)
        for c, (kind, gain_at) in enumerate(chunks[first:first + chunks_per_dot]):
            sl = slice(base + c * HEAD_DIM, base + (c + 1) * HEAD_DIM)
            gain = None if gain_at is None else gains_ref[:, gain_at:gain_at + HEAD_DIM]
            o_ref[:, sl] = _epilogue(kind, acc[:, c * HEAD_DIM:(c + 1) * HEAD_DIM], gain,
                                     cos_ref, sin_ref).astype(o_ref.dtype)


def _in_proj(x, gain, w, gains, cos, sin, chunks, *, seq, tm=512):
    t, d = x.shape
    n = w.shape[1]
    tm = min(tm, seq)
    assert t % tm == 0 and seq % tm == 0 and n == HEAD_DIM * len(chunks) and n % PROJ_TN == 0
    tab_blocks = seq // tm
    return pl.pallas_call(
        functools.partial(_in_proj_kernel, chunks=chunks),
        out_shape=jax.ShapeDtypeStruct((t, n), BF16),
        grid=(t // tm,),
        in_specs=[
            pl.BlockSpec((tm, d), lambda i: (i, 0)),
            pl.BlockSpec((1, d), lambda i: (0, 0)),
            pl.BlockSpec((d, n), lambda i: (0, 0), pipeline_mode=pl.Buffered(1)),
            pl.BlockSpec((1, gains.shape[1]), lambda i: (0, 0)),
            pl.BlockSpec((tm, HEAD_DIM), lambda i: (i % tab_blocks, 0)),
            pl.BlockSpec((tm, HEAD_DIM), lambda i: (i % tab_blocks, 0)),
        ],
        out_specs=pl.BlockSpec((tm, n), lambda i: (i, 0)),
        scratch_shapes=[pltpu.VMEM((tm, d), BF16)],
        compiler_params=_params("parallel"),
        name="in_proj",
    )(x, gain, w, gains, cos, sin)


def _matmul_residual_kernel(y_ref, w_ref, x_ref, o_ref):
    o_ref[...] = x_ref[...] + jnp.dot(y_ref[...], w_ref[...], preferred_element_type=F32)


def _matmul_residual(y, w, x, *, tm=512):
    t, k = y.shape
    n = w.shape[1]
    tm = min(tm, t)
    assert t % tm == 0
    return pl.pallas_call(
        _matmul_residual_kernel,
        out_shape=jax.ShapeDtypeStruct((t, n), F32),
        grid=(t // tm,),
        in_specs=[
            pl.BlockSpec((tm, k), lambda i: (i, 0)),
            pl.BlockSpec((k, n), lambda i: (0, 0)),
            pl.BlockSpec((tm, n), lambda i: (i, 0)),
        ],
        out_specs=pl.BlockSpec((tm, n), lambda i: (i, 0)),
        compiler_params=_params("parallel"),
        name="matmul_residual",
    )(y, w, x)


def _even_mixer_kernel(u_ref, vn_ref, ga_ref, q_ref, gb_ref,
                       kp_ref, kc_ref, kn_ref, vp_ref, vc_ref, vnx_ref,
                       ws_ref, bs_ref, sink_ref, o_ref, *, nb, blocks):
    t = pl.program_id(1)
    tok = lambda c: slice(c * BLOCK, (c + 1) * BLOCK)

    for g in range(A_GROUPS):
        sl = slice(g * HEAD_DIM, (g + 1) * HEAD_DIM)
        vn_chunks = jnp.concatenate([vn_ref[tok(c), sl] for c in range(blocks)], axis=1)
        mixed = jnp.dot(ws_ref[g], vn_chunks, preferred_element_type=F32)
        for c in range(blocks):
            mixed_c = mixed[:, tok(c)] + bs_ref[g]
            o_ref[tok(c), sl] = (u_ref[tok(c), sl].astype(F32) * mixed_c
                                 * ga_ref[tok(c), sl].astype(F32)).astype(o_ref.dtype)

    k_all = jnp.concatenate([kp_ref[...], kc_ref[...], kn_ref[...]], axis=0)
    v_all = jnp.concatenate([vp_ref[...], vc_ref[...], vnx_ref[...]], axis=0)
    qi = lax.broadcasted_iota(jnp.int32, (BLOCK, 3 * BLOCK), 0)
    kj = lax.broadcasted_iota(jnp.int32, (BLOCK, 3 * BLOCK), 1)
    band = (kj - qi >= 0) & (kj - qi <= 2 * BLOCK)

    for c in range(blocks):
        block_id = t * blocks + c
        first_key = jnp.where(block_id > 0, 0, BLOCK)
        end_key = jnp.where(block_id < nb - 1, 3 * BLOCK, 2 * BLOCK)
        mask = band & (kj >= first_key) & (kj < end_key)
        win = slice(c * BLOCK, (c + 3) * BLOCK)
        for h in range(B_KV_HEADS):
            ksl = slice(h * HEAD_DIM, (h + 1) * HEAD_DIM)
            q_stack = jnp.concatenate(
                [q_ref[tok(c), (h * B_REP + r) * HEAD_DIM:(h * B_REP + r + 1) * HEAD_DIM] for r in range(B_REP)],
                axis=0)
            s = lax.dot_general(q_stack, k_all[win, ksl], (((1,), (1,)), ((), ())),
                                preferred_element_type=F32)
            p_rows, inv_rows = [], []
            for r in range(B_REP):
                sink = sink_ref[h * B_REP + r] * LOG2_E
                s_r = jnp.where(mask, s[r * BLOCK:(r + 1) * BLOCK, :], NEG_BIG)
                m = jnp.maximum(jnp.max(s_r, axis=1, keepdims=True), sink)
                p = jnp.exp2(s_r - m)
                denom = jnp.sum(p, axis=1, keepdims=True) + jnp.exp2(sink - m)
                p_rows.append(p.astype(BF16))
                inv_rows.append(1.0 / denom)
            p_stack = jnp.concatenate(p_rows, axis=0)
            o = jnp.dot(p_stack, v_all[win, ksl], preferred_element_type=F32)
            for r in range(B_REP):
                col = (h * B_REP + r) * HEAD_DIM
                o_r = o[r * BLOCK:(r + 1) * BLOCK, :] * inv_rows[r]
                o_ref[tok(c), A_WIDTH + col:A_WIDTH + col + HEAD_DIM] = (
                    o_r * gb_ref[tok(c), col:col + HEAD_DIM].astype(F32)).astype(o_ref.dtype)


def _even_mixer(z, a_ws, a_bs, sink, *, blocks=4):
    b, s, _ = z.shape
    nb = s // BLOCK
    blocks = min(blocks, nb)
    assert nb % blocks == 0
    rows = blocks * BLOCK
    kv_w = B_KV_HEADS * HEAD_DIM
    wide = lambda c: pl.BlockSpec((None, rows, A_WIDTH), lambda bi, ti: (bi, ti, c))
    prev = lambda c: pl.BlockSpec((None, BLOCK, kv_w), lambda bi, ti: (bi, jnp.maximum(ti * blocks - 1, 0), c))
    nxt = lambda c: pl.BlockSpec((None, BLOCK, kv_w), lambda bi, ti: (bi, jnp.minimum((ti + 1) * blocks, nb - 1), c))
    cur = lambda c: pl.BlockSpec((None, rows, kv_w), lambda bi, ti: (bi, ti, c))
    full = lambda shape: pl.BlockSpec(shape, lambda bi, ti: (0,) * len(shape))
    in_specs = [
        wide(_EVEN_U), wide(_EVEN_V), wide(_EVEN_GA), wide(_EVEN_Q), wide(_EVEN_GB),
        prev(_EVEN_K), cur(_EVEN_K), nxt(_EVEN_K),
        prev(_EVEN_VB), cur(_EVEN_VB), nxt(_EVEN_VB),
        full((A_GROUPS, BLOCK, BLOCK)), full((A_GROUPS, BLOCK, HEAD_DIM)),
        pl.BlockSpec(memory_space=pltpu.SMEM),
    ]
    return pl.pallas_call(
        functools.partial(_even_mixer_kernel, nb=nb, blocks=blocks),
        out_shape=jax.ShapeDtypeStruct((b, s, A_WIDTH + B_WIDTH), BF16),
        grid=(b, nb // blocks),
        in_specs=in_specs,
        out_specs=pl.BlockSpec((None, rows, A_WIDTH + B_WIDTH), lambda bi, ti: (bi, ti, 0)),
        compiler_params=_params("parallel", "parallel"),
        name="even_mixer",
    )(z, z, z, z, z, z, z, z, z, z, z, a_ws, a_bs, sink)


def _flash_kernel(q_ref, q_next_ref, k_ref, v_ref, g_ref, o_ref,
                  qs_ref, qs_next_ref, s_ref, m_ref, acc_ref, *, tq, tk, n_kv):
    qi = pl.program_id(2)
    n_lane_tiles = tk // HEAD_DIM
    ones = jnp.ones((tk, HEAD_DIM), BF16)

    def stack_heads(src_ref, dst_ref):
        for r in range(C_REP):
            dst_ref[r * tq:(r + 1) * tq, :] = src_ref[:, r * HEAD_DIM:(r + 1) * HEAD_DIM]

    def scores(stacked_ref, j):
        start = pl.multiple_of(j * tk, tk)
        return lax.dot_general(stacked_ref[...], k_ref[pl.ds(start, tk), :], (((1,), (1,)), ((), ())),
                               preferred_element_type=F32)

    @pl.when(qi == 0)
    def _():
        stack_heads(q_ref, qs_ref)
        s_ref[0] = scores(qs_ref, 0)

    @pl.when(qi != 0)
    def _():
        qs_ref[...] = qs_next_ref[...]

    def softmax_pv(slot, j, first):
        start = pl.multiple_of(j * tk, tk)
        v_ext = jnp.concatenate([v_ref[pl.ds(start, tk), :], ones], axis=1)
        tiles = [s_ref[slot, :, c * HEAD_DIM:(c + 1) * HEAD_DIM] for c in range(n_lane_tiles)]
        m_cur = jnp.max(functools.reduce(jnp.maximum, tiles), axis=1, keepdims=True)
        m_new = jnp.broadcast_to(m_cur, m_ref.shape) if first else jnp.maximum(m_ref[...], m_cur)
        p = jnp.concatenate([jnp.exp2(tile - m_new).astype(BF16) for tile in tiles], axis=1)
        pv = jnp.dot(p, v_ext, preferred_element_type=F32)
        if first:
            acc_ref[...] = pv
        else:
            alpha = jnp.exp2(m_ref[...] - m_new)
            acc_ref[...] = jnp.concatenate([alpha, alpha], axis=1) * acc_ref[...] + pv
        m_ref[...] = m_new

    def run_pair(j0, first, last):
        s_ref[1] = scores(qs_ref, j0 + 1)
        softmax_pv(0, j0, first)
        if last:
            stack_heads(q_next_ref, qs_next_ref)
            s_ref[0] = scores(qs_next_ref, 0)
        else:
            s_ref[0] = scores(qs_ref, j0 + 2)
        softmax_pv(1, j0 + 1, False)

    def body(jj, carry):
        run_pair(2 * jj, False, False)
        return carry

    n_pairs = n_kv // 2
    run_pair(0, True, n_pairs == 1)
    lax.fori_loop(1, n_pairs - 1, body, 0)
    if n_pairs > 1:
        run_pair(n_kv - 2, False, True)

    for r in range(C_REP):
        sl = slice(r * HEAD_DIM, (r + 1) * HEAD_DIM)
        rows = slice(r * tq, (r + 1) * tq)
        o_r = acc_ref[rows, :HEAD_DIM] / acc_ref[rows, HEAD_DIM:]
        o_ref[:, sl] = (o_r * g_ref[:, sl].astype(F32)).astype(o_ref.dtype)


def _flash(z, *, tq=512, tk=512):
    b, s, _ = z.shape
    tq, tk = min(tq, s), min(tk, s)
    n_kv = s // tk
    assert s % tq == 0 and s % tk == 0 and n_kv % 2 == 0
    qw = C_REP * HEAD_DIM
    rows = C_REP * tq
    nq = s // tq
    return pl.pallas_call(
        functools.partial(_flash_kernel, tq=tq, tk=tk, n_kv=n_kv),
        out_shape=jax.ShapeDtypeStruct((b, s, C_WIDTH), BF16),
        grid=(b, C_KV_HEADS, nq),
        in_specs=[
            pl.BlockSpec((None, tq, qw), lambda bi, hi, qi: (bi, qi, hi)),
            pl.BlockSpec((None, tq, qw), lambda bi, hi, qi: (bi, jnp.minimum(qi + 1, nq - 1), hi)),
            pl.BlockSpec((None, s, HEAD_DIM), lambda bi, hi, qi: (bi, 0, _ODD_K_BLK128 + hi)),
            pl.BlockSpec((None, s, HEAD_DIM), lambda bi, hi, qi: (bi, 0, _ODD_V_BLK128 + hi)),
            pl.BlockSpec((None, tq, qw), lambda bi, hi, qi: (bi, qi, _ODD_G_BLK512 + hi)),
        ],
        out_specs=pl.BlockSpec((None, tq, qw), lambda bi, hi, qi: (bi, qi, hi)),
        scratch_shapes=[
            pltpu.VMEM((rows, HEAD_DIM), BF16),
            pltpu.VMEM((rows, HEAD_DIM), BF16),
            pltpu.VMEM((2, rows, tk), F32),
            pltpu.VMEM((rows, HEAD_DIM), F32),
            pltpu.VMEM((rows, 2 * HEAD_DIM), F32),
        ],
        compiler_params=_params("parallel", "parallel", "arbitrary"),
        name="flash_gqa",
    )(z, z, z, z, z)


def _reorder_heads(a, segments):
    lead = a.shape[:-1]
    x = a.reshape(lead + (a.shape[-1] // HEAD_DIM, HEAD_DIM))
    x = jnp.concatenate([x[..., lo:hi] for lo, hi in segments], axis=-1)
    return x.reshape(a.shape)


def _rope_tables(lane_pos, theta, half, rotated):
    lane = np.arange(HEAD_DIM)
    inv = jnp.power(F32(theta), -jnp.asarray(lane % half, F32) * (1.0 / half))
    ang = lane_pos * inv[None, :]
    sign = jnp.asarray(np.where(lane < HEAD_DIM // 2, -1.0, 1.0), F32)[None, :]
    rot = jnp.asarray(rotated)[None, :]
    return jnp.where(rot, jnp.cos(ang), 1.0), jnp.where(rot, jnp.sin(ang) * sign, 0.0)


def _trunk(x, tables, even_w, odd_w):
    b, s, d = x.shape
    (cos_b, sin_b), (cos_c, sin_c) = tables
    cos_b, sin_b, cos_c, sin_c = cos_b[:s], sin_b[:s], cos_c[:s], sin_c[:s]
    xf = x.reshape(b * s, d)
    for layer in range(len(even_w) + len(odd_w)):
        i = layer // 2
        if layer % 2 == 0:
            norm_g, w_in, gains, w_out, a_ws, a_bs, sink = even_w[i]
            z = _in_proj(xf, norm_g, w_in, gains, cos_b, sin_b, _EVEN_CHUNKS, seq=s).reshape(b, s, AB_IN)
            y = _even_mixer(z, a_ws, a_bs, sink)
        else:
            norm_g, w_in, gains, w_out = odd_w[i]
            z = _in_proj(xf, norm_g, w_in, gains, cos_c, sin_c, _ODD_CHUNKS, seq=s).reshape(b, s, C_IN)
            y = _flash(z)
        xf = _matmul_residual(y.reshape(b * s, d), w_out, xf)
    return xf.reshape(b, s, d)


def kernel(x_prompt, x_sample, norm_ab, w_in_ab, w_out_ab, a_v_norm, a_w_s, a_b_s, b_q_norm, b_k_norm,
           b_sink, norm_c, w_in_c, w_out_c, c_q_norm, c_k_norm):
    s_max = max(x_prompt.shape[1], x_sample.shape[1])
    pos = jnp.arange(s_max, dtype=F32)
    row = jnp.floor(pos / GRID_W)
    col = pos - row * GRID_W
    lane = np.arange(HEAD_DIM)
    row_lane = (lane // (HEAD_DIM // 4)) % 2 == 0
    tables = (
        _rope_tables(pos[:, None], ROPE_THETA, ROPE_DIMS // 2, lane % (HEAD_DIM // 2) < ROPE_DIMS // 2),
        _rope_tables(jnp.where(jnp.asarray(row_lane)[None, :], row[:, None], col[:, None]),
                     AXIAL_THETA, HEAD_DIM // 4, np.ones(HEAD_DIM, bool)),
    )

    kv_b = B_KV_HEADS * HEAD_DIM
    kv_c = C_KV_HEADS * HEAD_DIM
    c0 = 3 * A_WIDTH
    c1 = c0 + B_WIDTH
    c2 = c1 + kv_b
    c3 = c2 + kv_b
    even_w = []
    for i in range(norm_ab.shape[0]):
        w = w_in_ab[i].astype(BF16)
        w_cols = jnp.concatenate([
            w[:, :c0], _reorder_heads(w[:, c0:c1], _PARTIAL_SEGMENTS), w[:, c3:],
            _reorder_heads(w[:, c1:c2], _PARTIAL_SEGMENTS), w[:, c2:c3]], axis=1)
        q_gain = _reorder_heads(b_q_norm[i], _PARTIAL_SEGMENTS) * Q_SCALE
        k_gain = _reorder_heads(b_k_norm[i], _PARTIAL_SEGMENTS)
        gains = jnp.concatenate([a_v_norm[i], q_gain, k_gain])[None, :]
        even_w.append((
            norm_ab[i][None, :], w_cols, gains, w_out_ab[i].astype(BF16), a_w_s[i].astype(BF16),
            jnp.broadcast_to(a_b_s[i][:, :, None], (A_GROUPS, BLOCK, HEAD_DIM)), b_sink[i],
        ))
    odd_w = []
    for i in range(norm_c.shape[0]):
        w = w_in_c[i].astype(BF16)
        qk_end = C_WIDTH + kv_c
        w_cols = jnp.concatenate([_reorder_heads(w[:, :qk_end], _AXIAL_SEGMENTS), w[:, qk_end:]], axis=1)
        q_gain = _reorder_heads(c_q_norm[i], _AXIAL_SEGMENTS) * Q_SCALE
        k_gain = _reorder_heads(c_k_norm[i], _AXIAL_SEGMENTS)
        gains = jnp.concatenate([q_gain, k_gain])[None, :]
        odd_w.append((norm_c[i][None, :], w_cols, gains, w_out_c[i].astype(BF16)))

    y_prompt = _trunk(x_prompt, tables, even_w, odd_w)
    y_sample = _trunk(x_sample, tables, even_w, odd_w)
    return (y_prompt, y_sample)
```
